```python
import math, functools
import jax, jax.numpy as jnp
from jax import lax
import numpy as np

D_MODEL = 2048
BATCH = 2
SEQ = 4096
DEPTH = 2
DEC_BATCH = 128
DEC_SEQ = 1
PAST_LEN = 2048
PAGE_SIZE = 128

MIX_WIDTH = D_MODEL
ATT_WIDTH = MIX_WIDTH // 2
M_WIDTH = MIX_WIDTH - ATT_WIDTH
ATT_HEAD_DIM = 64
N_ATT_HEADS = ATT_WIDTH // (2 * ATT_HEAD_DIM)
ATT_V_DIM = 2 * ATT_HEAD_DIM
ATT_SCALE = ATT_HEAD_DIM ** -0.5
ROT_DIM = ATT_HEAD_DIM // 4
ROPE_THETA = 500000.0
N_M_HEADS = 4
M_HEAD_DIM = M_WIDTH // N_M_HEADS
M_CHUNK = 64
D_FF = ((8 * D_MODEL // 3 + 127) // 128) * 128
Q_BLOCK = 128
EPS = 1e-6
N_IN = 3 * ATT_WIDTH + 4 * M_WIDTH + 2 * N_M_HEADS

kernel_name = 'hybrid_diffattn_mlstm_macaron_step'


def rmsnorm(x, g):
    xf = x.astype(jnp.float32)
    y = xf * lax.rsqrt(jnp.mean(xf * xf, axis=-1, keepdims=True) + EPS)
    return (y * g.astype(jnp.float32)).astype(x.dtype)


def swiglu(x, w_in, w_out):
    gate, up = jnp.split(x @ w_in, 2, axis=-1)
    return (jax.nn.silu(gate) * up) @ w_out


def rope(x, pos):
    half = ROT_DIM // 2
    inv = ROPE_THETA ** (-jnp.arange(0, ROT_DIM, 2, dtype=jnp.float32) / ROT_DIM)
    ang = pos[:, None] * inv[None, :]
    cos = jnp.cos(ang)[None, :, None, :]
    sin = jnp.sin(ang)[None, :, None, :]
    xr = x[..., :ROT_DIM].astype(jnp.float32)
    x1, x2 = xr[..., :half], xr[..., half:]
    rot = jnp.concatenate([x1 * cos - x2 * sin, x2 * cos + x1 * sin], axis=-1).astype(x.dtype)
    return jnp.concatenate([rot, x[..., ROT_DIM:]], axis=-1)


def diff_weights(s, mask, lam):
    s = jnp.where(mask, s.astype(jnp.float32) * ATT_SCALE, -jnp.inf)
    p = jax.nn.softmax(s, axis=-1)
    return p[:, :, 0] - lam * p[:, :, 1]


def diff_attn_prompt(q, k, v, lam):
    B, S = q.shape[:2]
    nb = S // Q_BLOCK
    q_blocks = jnp.moveaxis(q.reshape(B, nb, Q_BLOCK, N_ATT_HEADS, 2, ATT_HEAD_DIM), 1, 0)
    k_pos = jnp.arange(S)

    def one_block(args):
        qb, i = args
        q_pos = i * Q_BLOCK + jnp.arange(Q_BLOCK)
        s = jnp.einsum('bqhcd,bkhcd->bhcqk', qb, k)
        a = diff_weights(s, k_pos[None, :] <= q_pos[:, None], lam)
        return jnp.einsum('bhqk,bkhe->bqhe', a.astype(v.dtype), v)

    o = lax.map(one_block, (q_blocks, jnp.arange(nb)))
    return jnp.moveaxis(o, 0, 1).reshape(B, S, N_ATT_HEADS, ATT_V_DIM)


def diff_attn_sample(q, k_new, v_new, lam, k_past, v_past):
    T = q.shape[1]
    P = k_past.shape[1]
    k = jnp.concatenate([k_past.astype(k_new.dtype), k_new], axis=1)
    v = jnp.concatenate([v_past.astype(v_new.dtype), v_new], axis=1)
    k_idx = jnp.arange(P + T)
    q_idx = P + jnp.arange(T)
    s = jnp.einsum('bqhcd,bkhcd->bhcqk', q, k)
    a = diff_weights(s, k_idx[None, :] <= q_idx[:, None], lam)
    return jnp.einsum('bhqk,bkhe->bqhe', a.astype(v.dtype), v)


def mlstm_chunkwise(q, k, v, ig, lf, C0, n0, m0):
    B, L, H, dh = q.shape
    c = math.gcd(L, M_CHUNK)
    nc = L // c

    def chunks(t):
        return jnp.moveaxis(t.astype(jnp.float32).reshape((B, nc, c) + t.shape[2:]), 1, 0)

    causal = jnp.tril(jnp.ones((c, c), dtype=bool))[None, :, :, None]

    def step(carry, inp):
        C, n, m = carry
        qc, kc, vc, ic, fc = inp
        b = jnp.cumsum(fc, axis=1)
        D = b[:, :, None, :] - b[:, None, :, :] + ic[:, None, :, :]
        D = jnp.where(causal, D, -jnp.inf)
        inter = b + m[:, None, :]
        m_t = jnp.maximum(inter, jnp.max(D, axis=2))
        w_intra = jnp.exp(D - m_t[:, :, None, :])
        w_inter = jnp.exp(inter - m_t)
        A = jnp.einsum('bthd,bshd->btsh', qc, kc) * w_intra
        num = jnp.einsum('btsh,bshd->bthd', A, vc) + w_inter[..., None] * jnp.einsum('bthd,bhde->bthe', qc, C)
        den = jnp.sum(A, axis=2) + w_inter * jnp.einsum('bthd,bhd->bth', qc, n)
        den = jnp.maximum(jnp.abs(den), jnp.exp(-m_t))
        h = num / den[..., None]
        m_new = m_t[:, -1]
        w_s = jnp.exp(b[:, -1:, :] - b + ic - m_new[:, None, :])
        decay = jnp.exp(b[:, -1] + m - m_new)
        C_new = decay[..., None, None] * C + jnp.einsum('bsh,bshd,bshe->bhde', w_s, kc, vc)
        n_new = decay[..., None] * n + jnp.einsum('bsh,bshd->bhd', w_s, kc)
        return (C_new, n_new, m_new), h

    init = (C0.astype(jnp.float32), n0.astype(jnp.float32), m0.astype(jnp.float32))
    (C, n, m), h = lax.scan(step, init, (chunks(q), chunks(k), chunks(v), chunks(ig), chunks(lf)))
    h = jnp.moveaxis(h, 0, 1).reshape(B, L, H, dh).astype(q.dtype)
    return h, C, n, m


def split_points():
    a, m, g = ATT_WIDTH, M_WIDTH, N_M_HEADS
    return [a, 2 * a, 3 * a, 3 * a + m, 3 * a + 2 * m, 3 * a + 3 * m, 3 * a + 4 * m, 3 * a + 4 * m + g]


def token_mixer(xn, pos, attend, C0, n0, m0, w_in, b_gate, lq1, lk1, lq2, lk2, subln, m_gain, w_out, lam_init):
    B, L, _ = xn.shape
    aq, ak, av, mq, mk, mv, mo, mi, mf = jnp.split(xn @ w_in, split_points(), axis=-1)
    aq = rope(aq.reshape(B, L, 2 * N_ATT_HEADS, ATT_HEAD_DIM), pos).reshape(B, L, N_ATT_HEADS, 2, ATT_HEAD_DIM)
    ak = rope(ak.reshape(B, L, 2 * N_ATT_HEADS, ATT_HEAD_DIM), pos).reshape(B, L, N_ATT_HEADS, 2, ATT_HEAD_DIM)
    av = av.reshape(B, L, N_ATT_HEADS, ATT_V_DIM)
    f32 = jnp.float32
    lam = (jnp.exp(jnp.sum(lq1.astype(f32) * lk1.astype(f32)))
           - jnp.exp(jnp.sum(lq2.astype(f32) * lk2.astype(f32))) + lam_init)
    a = attend(aq, ak, av, lam)
    a = (rmsnorm(a, subln) * (1.0 - lam_init)).reshape(B, L, ATT_WIDTH)
    shp = (B, L, N_M_HEADS, M_HEAD_DIM)
    ig = (mi + b_gate[:N_M_HEADS]).astype(f32)
    lf = jax.nn.log_sigmoid((mf + b_gate[N_M_HEADS:]).astype(f32))
    h, C, n, m = mlstm_chunkwise(mq.reshape(shp), mk.reshape(shp) * (M_HEAD_DIM ** -0.5), mv.reshape(shp),
                                 ig, lf, C0, n0, m0)
    h = rmsnorm(h, m_gain.reshape(N_M_HEADS, M_HEAD_DIM)) * jax.nn.sigmoid(mo).reshape(shp)
    y = jnp.concatenate([a, h.reshape(B, L, M_WIDTH)], axis=-1) @ w_out
    return y, ak.reshape(B, L, N_ATT_HEADS, 2 * ATT_HEAD_DIM), av, C, n, m


def decoder_layer(x, pos, attend, C0, n0, m0, g_ffn1, w1_in, w1_out, g_mix, w_in, b_gate,
                  lq1, lk1, lq2, lk2, subln, m_gain, w_out, g_ffn2, w2_in, w2_out, lam_init):
    x = x + 0.5 * swiglu(rmsnorm(x, g_ffn1), w1_in, w1_out)
    y, k_rows, v_rows, C, n, m = token_mixer(rmsnorm(x, g_mix), pos, attend, C0, n0, m0, w_in, b_gate,
                                             lq1, lk1, lq2, lk2, subln, m_gain, w_out, lam_init)
    x = x + y
    x = x + 0.5 * swiglu(rmsnorm(x, g_ffn2), w2_in, w2_out)
    return x, k_rows, v_rows, C, n, m


def setup_inputs(seed: int = 0) -> dict:
    key = jax.random.key(seed)
    ks = iter(jax.random.split(key, 40))
    f32 = jnp.float32

    def nrm(shape, scale):
        return scale * jax.random.normal(next(ks), shape, f32)

    def gain(shape):
        return 1.0 + nrm(shape, 0.01)

    n_pages = PAST_LEN // PAGE_SIZE
    n_used = DEC_BATCH * n_pages
    n_pool = n_used + max(1, n_used // 4)
    H, dh = N_M_HEADS, M_HEAD_DIM
    x_prompt = nrm((BATCH, SEQ, D_MODEL), 1.0)
    x_sample = nrm((DEC_BATCH, DEC_SEQ, D_MODEL), 1.0)
    cache_k = nrm((DEPTH, n_pool, PAGE_SIZE, N_ATT_HEADS, 2 * ATT_HEAD_DIM), 1.0)
    cache_v = nrm((DEPTH, n_pool, PAGE_SIZE, N_ATT_HEADS, ATT_V_DIM), 1.0)
    state_C = nrm((DEPTH, DEC_BATCH, H, dh, dh), 0.1)
    state_n = nrm((DEPTH, DEC_BATCH, H, dh), 0.1)
    state_m = nrm((DEPTH, DEC_BATCH, H), 1.0)
    page_table = jax.random.permutation(next(ks), n_pool)[:n_used].reshape(DEC_BATCH, n_pages).astype(jnp.int32)
    norm_ffn1 = gain((DEPTH, D_MODEL))
    ffn1_w_in = nrm((DEPTH, D_MODEL, 2 * D_FF), D_MODEL ** -0.5)
    ffn1_w_out = nrm((DEPTH, D_FF, D_MODEL), D_FF ** -0.5)
    norm_mix = gain((DEPTH, D_MODEL))
    w_mix_in = nrm((DEPTH, D_MODEL, N_IN), D_MODEL ** -0.5)
    b_gates = jnp.concatenate([nrm((DEPTH, H), 0.1),
                               jnp.linspace(3.0, 6.0, H, dtype=f32)[None, :] + nrm((DEPTH, H), 0.1)], axis=-1)
    lam_q1 = nrm((DEPTH, ATT_HEAD_DIM), 0.1)
    lam_k1 = nrm((DEPTH, ATT_HEAD_DIM), 0.1)
    lam_q2 = nrm((DEPTH, ATT_HEAD_DIM), 0.1)
    lam_k2 = nrm((DEPTH, ATT_HEAD_DIM), 0.1)
    attn_subln = gain((DEPTH, ATT_V_DIM))
    mlstm_gain = gain((DEPTH, M_WIDTH))
    w_mix_out = nrm((DEPTH, MIX_WIDTH, D_MODEL), MIX_WIDTH ** -0.5)
    norm_ffn2 = gain((DEPTH, D_MODEL))
    ffn2_w_in = nrm((DEPTH, D_MODEL, 2 * D_FF), D_MODEL ** -0.5)
    ffn2_w_out = nrm((DEPTH, D_FF, D_MODEL), D_FF ** -0.5)
    norm_final = gain((D_MODEL,))
    return {'x_prompt': x_prompt, 'x_sample': x_sample, 'cache_k': cache_k, 'cache_v': cache_v,
            'state_C': state_C, 'state_n': state_n, 'state_m': state_m, 'page_table': page_table,
            'norm_ffn1': norm_ffn1, 'ffn1_w_in': ffn1_w_in, 'ffn1_w_out': ffn1_w_out,
            'norm_mix': norm_mix, 'w_mix_in': w_mix_in, 'b_gates': b_gates,
            'lam_q1': lam_q1, 'lam_k1': lam_k1, 'lam_q2': lam_q2, 'lam_k2': lam_k2,
            'attn_subln': attn_subln, 'mlstm_gain': mlstm_gain, 'w_mix_out': w_mix_out,
            'norm_ffn2': norm_ffn2, 'ffn2_w_in': ffn2_w_in, 'ffn2_w_out': ffn2_w_out,
            'norm_final': norm_final}


def reference(x_prompt, x_sample, cache_k, cache_v, state_C, state_n, state_m, page_table,
              norm_ffn1, ffn1_w_in, ffn1_w_out, norm_mix, w_mix_in, b_gates,
              lam_q1, lam_k1, lam_q2, lam_k2, attn_subln, mlstm_gain, w_mix_out,
              norm_ffn2, ffn2_w_in, ffn2_w_out, norm_final):
    f32 = jnp.float32
    B, S, _ = x_prompt.shape
    DB, T, _ = x_sample.shape
    P = page_table.shape[1] * cache_k.shape[2]
    pos_p = jnp.arange(S, dtype=f32)
    pos_s = P + jnp.arange(T, dtype=f32)
    xp, xs = x_prompt, x_sample
    kp_l, vp_l, Cp_l, np_l, mp_l = [], [], [], [], []
    ks_l, vs_l, Cs_l, ns_l, ms_l = [], [], [], [], []
    for l in range(DEPTH):
        lam_init = 0.8 - 0.6 * math.exp(-0.3 * l)
        lp = (norm_ffn1[l], ffn1_w_in[l], ffn1_w_out[l], norm_mix[l], w_mix_in[l], b_gates[l],
              lam_q1[l], lam_k1[l], lam_q2[l], lam_k2[l], attn_subln[l], mlstm_gain[l], w_mix_out[l],
              norm_ffn2[l], ffn2_w_in[l], ffn2_w_out[l])
        C0 = jnp.zeros((B, N_M_HEADS, M_HEAD_DIM, M_HEAD_DIM), f32)
        n0 = jnp.zeros((B, N_M_HEADS, M_HEAD_DIM), f32)
        m0 = jnp.zeros((B, N_M_HEADS), f32)
        xp, k_r, v_r, C_r, n_r, m_r = decoder_layer(xp, pos_p, diff_attn_prompt, C0, n0, m0, *lp, lam_init)
        kp_l.append(k_r); vp_l.append(v_r); Cp_l.append(C_r); np_l.append(n_r); mp_l.append(m_r)
        k_past = cache_k[l][page_table].reshape(DB, P, N_ATT_HEADS, 2, ATT_HEAD_DIM)
        v_past = cache_v[l][page_table].reshape(DB, P, N_ATT_HEADS, ATT_V_DIM)
        attend_s = functools.partial(diff_attn_sample, k_past=k_past, v_past=v_past)
        xs, k_r, v_r, C_r, n_r, m_r = decoder_layer(xs, pos_s, attend_s, state_C[l], state_n[l], state_m[l], *lp, lam_init)
        ks_l.append(k_r); vs_l.append(v_r); Cs_l.append(C_r); ns_l.append(n_r); ms_l.append(m_r)
    y_prompt = rmsnorm(xp, norm_final)
    y_sample = rmsnorm(xs, norm_final)
    return (y_prompt, y_sample,
            jnp.stack(kp_l), jnp.stack(vp_l), jnp.stack(Cp_l), jnp.stack(np_l), jnp.stack(mp_l),
            jnp.stack(ks_l), jnp.stack(vs_l), jnp.stack(Cs_l), jnp.stack(ns_l), jnp.stack(ms_l))
```

```python
import functools
import math

import jax
import jax.numpy as jnp
import numpy as np
from jax import lax
from jax.experimental import pallas as pl
from jax.experimental.pallas import tpu as pltpu

F32 = jnp.float32
BF16 = jnp.bfloat16

EPS = 1e-6
ATT_HEAD_DIM = 64
ATT_V_DIM = 2 * ATT_HEAD_DIM
ATT_SCALE = ATT_HEAD_DIM ** -0.5
ROT_DIM = ATT_HEAD_DIM // 4
ROPE_THETA = 500000.0
N_M_HEADS = 4
N_GATES = 2 * N_M_HEADS

LANES = 128
SUBLANES = 8
VMEM_LIMIT_BYTES = 56 * 1024 * 1024

NEG_INF = float("-inf")


def _cparams(semantics):
    return pltpu.CompilerParams(dimension_semantics=semantics, vmem_limit_bytes=VMEM_LIMIT_BYTES)


def _rms(x):
    return x * lax.rsqrt(jnp.mean(x * x, axis=-1, keepdims=True) + EPS)


def _sigmoid(x):
    return 1.0 / (1.0 + jnp.exp(-x))


def _dot(a, b):
    return jnp.dot(a, b, preferred_element_type=F32)


def _dot_nt(a, b):
    return lax.dot_general(a, b, (((1,), (1,)), ((), ())), preferred_element_type=F32)


def _dot_tn(a, b):
    return lax.dot_general(a, b, (((0,), (0,)), ((), ())), preferred_element_type=F32)


def _split_hi_lo(x):
    hi = x.astype(BF16)
    lo = (x - hi.astype(F32)).astype(BF16)
    return hi, lo


def _ffn_kernel(x_ref, g_ref, wg_ref, wu_ref, wo_ref, gf_ref, o_ref, xn_ref, *, final_norm):
    j = pl.program_id(1)

    @pl.when(j == 0)
    def _():
        x = x_ref[...]
        xn_ref[...] = (_rms(x) * g_ref[...]).astype(BF16)
        o_ref[...] = x

    xn = xn_ref[...]
    gate = _dot(xn, wg_ref[...])
    up = _dot(xn, wu_ref[...])
    act = (gate * _sigmoid(gate)) * up * 0.5
    o_ref[...] += _dot(act.astype(BF16), wo_ref[...])

    if final_norm:
        @pl.when(j == pl.num_programs(1) - 1)
        def _():
            o_ref[...] = _rms(o_ref[...]) * gf_ref[...]


def _ffn(x, g, wg, wu, wo, gf, *, tm, tf, final_norm):
    m, d = x.shape
    fp = wg.shape[1]
    return pl.pallas_call(
        functools.partial(_ffn_kernel, final_norm=final_norm),
        grid=(m // tm, fp // tf),
        in_specs=[
            pl.BlockSpec((tm, d), lambda i, j: (i, 0)),
            pl.BlockSpec((1, d), lambda i, j: (0, 0)),
            pl.BlockSpec((d, tf), lambda i, j: (0, j)),
            pl.BlockSpec((d, tf), lambda i, j: (0, j)),
            pl.BlockSpec((tf, d), lambda i, j: (j, 0)),
            pl.BlockSpec((1, d), lambda i, j: (0, 0)),
        ],
        out_specs=pl.BlockSpec((tm, d), lambda i, j: (i, 0)),
        out_shape=jax.ShapeDtypeStruct((m, d), F32),
        scratch_shapes=[pltpu.VMEM((tm, d), BF16)],
        compiler_params=_cparams(("parallel", "arbitrary")),
    )(x, g, wg, wu, wo, gf)


def _prep_ffn_weights(w_in, w_out, tf):
    f = w_out.shape[0]
    fp = ((f + tf - 1) // tf) * tf
    wg = jnp.pad(w_in[:, :f].astype(BF16), ((0, 0), (0, fp - f)))
    wu = jnp.pad(w_in[:, f:].astype(BF16), ((0, 0), (0, fp - f)))
    wo = jnp.pad(w_out.astype(BF16), ((0, fp - f), (0, 0)))
    return wg, wu, wo


def _rope_table():
    half = ROT_DIM // 2
    inv = ROPE_THETA ** (-jnp.arange(0, ROT_DIM, 2, dtype=F32) / ROT_DIM)
    r = np.arange(LANES) % ATT_HEAD_DIM
    inv_l = jnp.where(r < ROT_DIM, inv[r % half], 0.0)
    tab = jnp.zeros((SUBLANES, LANES), F32)
    tab = tab.at[0].set(inv_l)
    tab = tab.at[1].set(jnp.asarray(np.where(r < half, -1.0, 0.0), F32))
    tab = tab.at[2].set(jnp.asarray(np.where((r >= half) & (r < ROT_DIM), 1.0, 0.0), F32))
    return tab


def _mixin_kernel(x_ref, g_ref, w_ref, wgt_ref, bg_ref, tab_ref,
                  proj_ref, qx_ref, kvb_ref, gt_ref,
                  xn_ref, cs_ref, *, tm, bw, seq, pos_offset, pos_step):
    i = pl.program_id(0)
    j = pl.program_id(1)
    half = ROT_DIM // 2

    @pl.when(j == 0)
    def _():
        xn = _rms(x_ref[...]) * g_ref[...]
        x_hi, x_lo = _split_hi_lo(xn)
        xn_ref[...] = x_hi
        row0 = ((i * tm) % seq) * pos_step + pos_offset
        pos = (row0 + pos_step * lax.broadcasted_iota(jnp.int32, (tm, LANES), 0)).astype(F32)
        ang = pos * tab_ref[0:1, :]
        sin = jnp.sin(ang)
        cs_ref[0] = jnp.cos(ang)
        cs_ref[1] = sin * tab_ref[1:2, :]
        cs_ref[2] = sin * tab_ref[2:3, :]
        wst = wgt_ref[...]
        r1 = _dot_nt(wst, x_hi)
        r2 = _dot_nt(wst, x_lo)
        pre = r1[:N_GATES] + r1[N_GATES:] + r2[:N_GATES] + bg_ref[...][:, :1]
        logsig = jnp.minimum(pre, 0.0) - jnp.log1p(jnp.exp(-jnp.abs(pre)))
        is_f = lax.broadcasted_iota(jnp.int32, pre.shape, 0) >= N_M_HEADS
        gt_ref[...] = jnp.where(is_f, logsig, pre)

    y = _dot(xn_ref[...], w_ref[...])

    def rotate(y):
        cols = []
        for c in range(bw // LANES):
            yc = y[:, c * LANES:(c + 1) * LANES]
            cols.append(yc * cs_ref[0]
                        + pltpu.roll(yc, LANES - half, 1) * cs_ref[1]
                        + pltpu.roll(yc, half, 1) * cs_ref[2])
        return cols

    @pl.when(j == 0)
    def _():
        cols = rotate(y)
        lane = lax.broadcasted_iota(jnp.int32, (tm, LANES), 1)
        first = lane < ATT_HEAD_DIM
        for c, yc in enumerate(cols):
            yc = yc * ATT_SCALE
            proj_ref[:, c * LANES:(c + 1) * LANES] = yc
            qx_ref[:, (2 * c) * LANES:(2 * c + 1) * LANES] = jnp.where(first, yc, 0.0).astype(BF16)
            qx_ref[:, (2 * c + 1) * LANES:(2 * c + 2) * LANES] = jnp.where(first, 0.0, yc).astype(BF16)

    @pl.when(j == 1)
    def _():
        cols = rotate(y)
        for c, yc in enumerate(cols):
            proj_ref[:, c * LANES:(c + 1) * LANES] = yc
            kvb_ref[:, c * LANES:(c + 1) * LANES] = yc.astype(BF16)

    @pl.when(j == 2)
    def _():
        proj_ref[...] = y
        kvb_ref[...] = y.astype(BF16)

    @pl.when(j > 2)
    def _():
        proj_ref[...] = y


def _mixin(x, g, w_main, wgt, bg, tab, *, tm, seq, pos_offset, pos_step):
    m, d = x.shape
    bw = d // 2
    nblk = w_main.shape[1] // bw
    kern = functools.partial(_mixin_kernel, tm=tm, bw=bw, seq=seq, pos_offset=pos_offset, pos_step=pos_step)
    return pl.pallas_call(
        kern,
        grid=(m // tm, nblk),
        in_specs=[
            pl.BlockSpec((tm, d), lambda i, j: (i, 0)),
            pl.BlockSpec((1, d), lambda i, j: (0, 0)),
            pl.BlockSpec((d, bw), lambda i, j: (0, j)),
            pl.BlockSpec((2 * N_GATES, d), lambda i, j: (0, 0)),
            pl.BlockSpec((N_GATES, LANES), lambda i, j: (0, 0)),
            pl.BlockSpec((SUBLANES, LANES), lambda i, j: (0, 0)),
        ],
        out_specs=[
            pl.BlockSpec((tm, bw), lambda i, j: (i, j)),
            pl.BlockSpec((tm, 2 * bw), lambda i, j: (i, 0)),
            pl.BlockSpec((tm, bw), lambda i, j: (i, jnp.clip(j - 1, 0, 1))),
            pl.BlockSpec((N_GATES, tm), lambda i, j: (0, i)),
        ],
        out_shape=[
            jax.ShapeDtypeStruct((m, nblk * bw), F32),
            jax.ShapeDtypeStruct((m, 2 * bw), BF16),
            jax.ShapeDtypeStruct((m, 2 * bw), BF16),
            jax.ShapeDtypeStruct((N_GATES, m), F32),
        ],
        scratch_shapes=[pltpu.VMEM((tm, d), BF16), pltpu.VMEM((3, tm, LANES), F32)],
        compiler_params=_cparams(("parallel", "arbitrary")),
    )(x, g, w_main, wgt, bg, tab)


def _mixout_kernel(x_ref, a_ref, h_ref, wa_ref, wh_ref, o_ref):
    o_ref[...] = (x_ref[...] + _dot(a_ref[...].astype(BF16), wa_ref[...])
                  + _dot(h_ref[...].astype(BF16), wh_ref[...]))


def _mixout(x, a, h, wa, wh, *, tm):
    m, d = x.shape
    bw = a.shape[1]
    return pl.pallas_call(
        _mixout_kernel,
        grid=(m // tm,),
        in_specs=[
            pl.BlockSpec((tm, d), lambda i: (i, 0)),
            pl.BlockSpec((tm, bw), lambda i: (i, 0)),
            pl.BlockSpec((tm, bw), lambda i: (i, 0)),
            pl.BlockSpec((bw, d), lambda i: (0, 0)),
            pl.BlockSpec((bw, d), lambda i: (0, 0)),
        ],
        out_specs=pl.BlockSpec((tm, d), lambda i: (i, 0)),
        out_shape=jax.ShapeDtypeStruct((m, d), F32),
        compiler_params=_cparams(("parallel",)),
    )(x, a, h, wa, wh)


def _lam_from(lam_ref, lam_init):
    lv = lam_ref[...]
    s1 = jnp.sum(lv[0:1] * lv[1:2], axis=-1, keepdims=True)
    s2 = jnp.sum(lv[2:3] * lv[3:4], axis=-1, keepdims=True)
    return jnp.exp(s1) - jnp.exp(s2) + lam_init


def _attn_prompt_kernel(lam_ref, subln_ref, q_ref, k_ref, v_ref, o_ref, m_scr, acc_scr, *, tq, lam_init):
    qi = pl.program_id(2)
    tk = tq
    q2 = q_ref[...]
    qs = jnp.concatenate([q2[:, :LANES], q2[:, LANES:]], axis=0)
    m_scr[...] = jnp.full(m_scr.shape, NEG_INF, F32)
    acc_scr[...] = jnp.zeros(acc_scr.shape, F32)
    ones = jnp.ones((tk, LANES), BF16)

    def chunk(j, masked):
        start = pl.multiple_of(j * tk, tk)
        kc = k_ref[pl.ds(start, tk), :]
        vc = v_ref[pl.ds(start, tk), :]
        s = _dot_nt(qs, kc)
        if masked:
            row = lax.broadcasted_iota(jnp.int32, s.shape, 0)
            row = jnp.where(row >= tq, row - tq, row)
            col = lax.broadcasted_iota(jnp.int32, s.shape, 1)
            s = jnp.where(col <= row, s, NEG_INF)
        m_prev = m_scr[...]
        m_new = jnp.maximum(m_prev, jnp.max(s, axis=1, keepdims=True))
        alpha = jnp.exp(m_prev - m_new)
        p = jnp.exp(s - pltpu.repeat(m_new, tk // LANES, 1))
        vx = jnp.concatenate([vc, ones], axis=1)
        acc_scr[...] = pltpu.repeat(alpha, 2, 1) * acc_scr[...] + _dot(p.astype(BF16), vx)
        m_scr[...] = m_new

    def body(j, carry):
        chunk(j, False)
        return carry

    lax.fori_loop(0, qi, body, 0)
    chunk(qi, True)

    acc = acc_scr[...]
    o1 = acc[:tq, :LANES] / acc[:tq, LANES:]
    o2 = acc[tq:, :LANES] / acc[tq:, LANES:]
    o = o1 - _lam_from(lam_ref, lam_init) * o2
    o_ref[...] = (_rms(o) * subln_ref[...] * (1.0 - lam_init)).astype(o_ref.dtype)


def _attn_prompt(lam, subln, qx, kvb, *, batch, seq, tq, lam_init):
    m = qx.shape[0]
    bw = kvb.shape[1] // 2
    nh = bw // ATT_V_DIM
    nq = seq // tq
    kern = functools.partial(_attn_prompt_kernel, tq=tq, lam_init=lam_init)
    return pl.pallas_call(
        kern,
        grid=(batch, nh, nq),
        in_specs=[
            pl.BlockSpec(lam.shape, lambda b, h, i: (0, 0)),
            pl.BlockSpec((1, ATT_V_DIM), lambda b, h, i: (0, 0)),
            pl.BlockSpec((tq, 2 * LANES), lambda b, h, i: (b * nq + i, h)),
            pl.BlockSpec((seq, LANES), lambda b, h, i: (b, h)),
            pl.BlockSpec((seq, LANES), lambda b, h, i: (b, nh + h)),
        ],
        out_specs=pl.BlockSpec((tq, ATT_V_DIM), lambda b, h, i: (b * nq + i, h)),
        out_shape=jax.ShapeDtypeStruct((m, bw), BF16),
        scratch_shapes=[pltpu.VMEM((2 * tq, LANES), F32), pltpu.VMEM((2 * tq, 2 * LANES), F32)],
        compiler_params=_cparams(("parallel", "parallel", "arbitrary")),
    )(lam, subln, qx, kvb, kvb)


def _attn_sample_kernel(pt_ref, lam_ref, subln_ref, qm_ref, kn_ref, vn_ref, *rest,
                        n_pages, page, nh, lam_init):
    k_refs = rest[:n_pages]
    v_refs = rest[n_pages:2 * n_pages]
    o_ref = rest[2 * n_pages]
    s_scr = rest[2 * n_pages + 1]
    rows = page * nh
    qm = qm_ref[0]

    def head_mask(width):
        lane = lax.broadcasted_iota(jnp.int32, (nh, width), 1)
        sub = lax.broadcasted_iota(jnp.int32, (nh, width), 0)
        return (lane % nh) == sub

    def scores(kflat, width):
        st = _dot_nt(qm, kflat)
        st = st.reshape(2, nh, width)
        return jnp.sum(jnp.where(head_mask(width)[None], st, 0.0), axis=1)

    for j in range(n_pages):
        kf = k_refs[j][0, 0].reshape(rows, LANES).astype(BF16)
        s_scr[0:2, j * rows:(j + 1) * rows] = scores(kf, rows)
    pad = jnp.zeros((LANES - nh, LANES), F32)
    knp = jnp.concatenate([kn_ref[0], pad], axis=0).astype(BF16)
    s_new = scores(knp, LANES)
    lane = lax.broadcasted_iota(jnp.int32, s_new.shape, 1)
    s_scr[0:2, n_pages * rows:] = jnp.where(lane < nh, s_new, NEG_INF)

    s_all = s_scr[0:2, :]
    ncol = s_all.shape[1] // LANES

    def per_head_allreduce(x, op):
        r = x[:, :LANES]
        for c in range(1, ncol):
            r = op(r, x[:, c * LANES:(c + 1) * LANES])
        sh = nh
        while sh < LANES:
            r = op(r, pltpu.roll(r, sh, 1))
            sh *= 2
        return r

    mx = per_head_allreduce(s_all, jnp.maximum)
    p = jnp.exp(s_all - pltpu.repeat(mx, ncol, 1))
    den = per_head_allreduce(p, jnp.add)
    pn = p / pltpu.repeat(den, ncol, 1)
    w = pn[0:1] - _lam_from(lam_ref, lam_init) * pn[1:2]

    def weighted(wj, vflat, width):
        a = jnp.where(head_mask(width), jnp.broadcast_to(wj, (nh, width)), 0.0)
        return _dot(a.astype(BF16), vflat)

    o = jnp.zeros((nh, LANES), F32)
    for j in range(n_pages):
        vf = v_refs[j][0, 0].reshape(rows, LANES).astype(BF16)
        o = o + weighted(w[:, j * rows:(j + 1) * rows], vf, rows)
    vnp = jnp.concatenate([vn_ref[0], pad], axis=0).astype(BF16)
    o = o + weighted(w[:, n_pages * rows:], vnp, LANES)
    o_ref[0] = _rms(o) * subln_ref[...] * (1.0 - lam_init)


def _attn_sample(page_table, lam, subln, qm, kn, vn, cache_k, cache_v, *, layer, lam_init):
    db, n_pages = page_table.shape
    _, _, page, nh, vd = cache_k.shape
    rows = page * nh

    def page_spec(j):
        return pl.BlockSpec((1, 1, page, nh, vd), lambda b, pt: (layer, pt[b, j], 0, 0, 0))

    kern = functools.partial(_attn_sample_kernel, n_pages=n_pages, page=page, nh=nh, lam_init=lam_init)
    grid_spec = pltpu.PrefetchScalarGridSpec(
        num_scalar_prefetch=1,
        grid=(db,),
        in_specs=[
            pl.BlockSpec(lam.shape, lambda b, pt: (0, 0)),
            pl.BlockSpec((1, vd), lambda b, pt: (0, 0)),
            pl.BlockSpec((1, 2 * nh, LANES), lambda b, pt: (b, 0, 0)),
            pl.BlockSpec((1, nh, vd), lambda b, pt: (b, 0, 0)),
            pl.BlockSpec((1, nh, vd), lambda b, pt: (b, 0, 0)),
        ] + [page_spec(j) for j in range(n_pages)] + [page_spec(j) for j in range(n_pages)],
        out_specs=pl.BlockSpec((1, nh, vd), lambda b, pt: (b, 0, 0)),
        scratch_shapes=[pltpu.VMEM((SUBLANES, n_pages * rows + LANES), F32)],
    )
    return pl.pallas_call(
        kern,
        grid_spec=grid_spec,
        out_shape=jax.ShapeDtypeStruct((db, nh, vd), F32),
        compiler_params=_cparams(("arbitrary",)),
    )(page_table, lam, subln, qm, kn, vn, *([cache_k] * n_pages), *([cache_v] * n_pages))


def _mlstm_prompt_kernel(q_ref, k_ref, v_ref, mo_ref, ig_ref, lf_ref, gain_ref,
                         h_ref, c_out_ref, n_out_ref, m_out_ref,
                         c_scr, n_scr, m_scr, *, c, dh):
    kk = pl.program_id(2)

    @pl.when(kk == 0)
    def _():
        c_scr[...] = jnp.zeros(c_scr.shape, F32)
        n_scr[...] = jnp.zeros(n_scr.shape, F32)
        m_scr[...] = jnp.zeros(m_scr.shape, F32)

    rows = jnp.concatenate([lf_ref[0], ig_ref[0], jnp.zeros((SUBLANES - 2, c), F32)], axis=0)
    r_hi, r_lo = _split_hi_lo(rows)
    ti = lax.broadcasted_iota(jnp.int32, (c, c), 0)
    si = lax.broadcasted_iota(jnp.int32, (c, c), 1)
    tril = (si <= ti)
    lower = tril.astype(BF16)
    upper = (ti <= si).astype(BF16)
    eye = (ti == si).astype(BF16)
    b_rows = _dot(r_hi, upper) + _dot(r_lo, upper)
    cum_cols = _dot_nt(lower, r_hi) + _dot_nt(lower, r_lo)
    id_cols = _dot_nt(eye, r_hi) + _dot_nt(eye, r_lo)
    b_row = b_rows[0:1, :]
    ig_row = ig_ref[0]
    b_col = cum_cols[:, 0:1]
    ig_col = id_cols[:, 1:2]
    b_last = b_col[c - 1:c, :]
    m_prev = m_scr[0:1, 0:1]

    d_mat = jnp.where(tril, b_col - b_row + ig_row, NEG_INF)
    inter = b_col + m_prev
    m_t = jnp.maximum(inter, jnp.max(d_mat, axis=1, keepdims=True))
    w_intra = jnp.exp(d_mat - m_t)
    w_inter = jnp.exp(inter - m_t)

    qb = q_ref[...].astype(BF16)
    kf = k_ref[...] * (dh ** -0.5)
    kb = kf.astype(BF16)
    vb = v_ref[...].astype(BF16)
    a = _dot_nt(qb, kb) * w_intra
    c_prev = c_scr[...]
    n_prev = n_scr[...]
    num = _dot(a.astype(BF16), vb) + w_inter * _dot(qb, c_prev.astype(BF16))
    qn = jnp.sum(qb.astype(F32) * n_prev.astype(BF16).astype(F32), axis=1, keepdims=True)
    den = jnp.sum(a, axis=1, keepdims=True) + w_inter * qn
    den = jnp.maximum(jnp.abs(den), jnp.exp(-m_t))
    hh = num / den

    m_new = m_t[c - 1:c, :]
    w_s = jnp.exp(b_last - b_col + ig_col - m_new)
    decay = jnp.exp(b_last + m_prev - m_new)
    kw = kf * w_s
    c_new = decay * c_prev + _dot_tn(kw.astype(BF16), vb)
    n_new = decay * n_prev + jnp.sum(kw, axis=0, keepdims=True)
    c_scr[...] = c_new
    n_scr[...] = n_new
    m_scr[...] = jnp.broadcast_to(m_new, m_scr.shape)
    c_out_ref[0, 0] = c_new
    n_out_ref[0, 0] = n_new
    m_out_ref[0, 0] = jnp.broadcast_to(m_new, (1, LANES))

    h_ref[...] = (_rms(hh) * gain_ref[...] * _sigmoid(mo_ref[...])).astype(h_ref.dtype)


def _mlstm_prompt(proj, gt3, gain, *, batch, seq, c):
    m = proj.shape[0]
    bw = proj.shape[1] // 7
    nmh = N_M_HEADS
    dh = bw // nmh
    nc = seq // c
    kern = functools.partial(_mlstm_prompt_kernel, c=c, dh=dh)

    def col(base):
        return lambda b, h, k: (b * nc + k, base * nmh + h)

    return pl.pallas_call(
        kern,
        grid=(batch, nmh, nc),
        in_specs=[
            pl.BlockSpec((c, dh), col(3)),
            pl.BlockSpec((c, dh), col(4)),
            pl.BlockSpec((c, dh), col(5)),
            pl.BlockSpec((c, dh), col(6)),
            pl.BlockSpec((1, 1, c), lambda b, h, k: (h, 0, b * nc + k)),
            pl.BlockSpec((1, 1, c), lambda b, h, k: (nmh + h, 0, b * nc + k)),
            pl.BlockSpec((1, dh), lambda b, h, k: (0, h)),
        ],
        out_specs=[
            pl.BlockSpec((c, dh), lambda b, h, k: (b * nc + k, h)),
            pl.BlockSpec((1, 1, dh, dh), lambda b, h, k: (b, h, 0, 0)),
            pl.BlockSpec((1, 1, 1, dh), lambda b, h, k: (b, h, 0, 0)),
            pl.BlockSpec((1, 1, 1, LANES), lambda b, h, k: (b, h, 0, 0)),
        ],
        out_shape=[
            jax.ShapeDtypeStruct((m, bw), BF16),
            jax.ShapeDtypeStruct((batch, nmh, dh, dh), F32),
            jax.ShapeDtypeStruct((batch, nmh, 1, dh), F32),
            jax.ShapeDtypeStruct((batch, nmh, 1, LANES), F32),
        ],
        scratch_shapes=[pltpu.VMEM((dh, dh), F32), pltpu.VMEM((1, dh), F32), pltpu.VMEM((SUBLANES, LANES), F32)],
        compiler_params=_cparams(("parallel", "parallel", "arbitrary")),
    )(proj, proj, proj, proj, gt3, gt3, gain)


def _mlstm_sample_kernel(p_ref, g_ref, c_ref, n_ref, m_ref, gain_ref,
                         h_ref, c_out_ref, n_out_ref, m_out_ref, *, bs, bw, dh):
    nmh = N_M_HEADS
    g = g_ref[...]
    m_all = m_ref[0]
    rowid = lax.broadcasted_iota(jnp.int32, (bs, dh), 0)
    for h in range(nmh):
        def cols(base):
            return p_ref[:, base * bw + h * dh: base * bw + (h + 1) * dh]
        q, k, v, mo = cols(3), cols(4) * (dh ** -0.5), cols(5), cols(6)
        ig = g[:, h:h + 1]
        lf = g[:, nmh + h:nmh + h + 1]
        m_prev = m_all[:, h:h + 1]
        m_t = jnp.maximum(lf + m_prev, ig)
        w_i = jnp.exp(ig - m_t)
        w_f = jnp.exp(lf + m_prev - m_t)
        qb = q.astype(BF16)
        qf = qb.astype(F32)
        kb = k.astype(BF16).astype(F32)
        vb = v.astype(BF16)
        n_prev = n_ref[0][:, h * dh:(h + 1) * dh]
        a = jnp.sum(qf * kb, axis=1, keepdims=True) * w_i
        qn = jnp.sum(qf * n_prev.astype(BF16).astype(F32), axis=1, keepdims=True)
        kw = k * w_i
        qc = jnp.zeros((bs, dh), F32)
        for j in range(bs):
            c_prev = c_ref[0, j, h]
            qc = jnp.where(rowid == j, _dot(qb, c_prev.astype(BF16)), qc)
            kwj = jnp.where(rowid == j, kw, 0.0).astype(BF16)
            c_out_ref[j, h] = w_f[j:j + 1, :] * c_prev + _dot_tn(kwj, vb)
        num = a * vb.astype(F32) + w_f * qc
        den = a + w_f * qn
        den = jnp.maximum(jnp.abs(den), jnp.exp(-m_t))
        hh = num / den
        n_out_ref[:, h * dh:(h + 1) * dh] = w_f * n_prev + kw
        m_out_ref[:, h:h + 1] = m_t
        gain = gain_ref[:, h * dh:(h + 1) * dh]
        h_ref[:, h * dh:(h + 1) * dh] = _rms(hh) * gain * _sigmoid(mo)


def _mlstm_sample(proj, g, state_c, state_n2, state_m, gain, *, layer, bs):
    db = proj.shape[0]
    bw = proj.shape[1] // 7
    nmh = N_M_HEADS
    dh = bw // nmh
    kern = functools.partial(_mlstm_sample_kernel, bs=bs, bw=bw, dh=dh)
    return pl.pallas_call(
        kern,
        grid=(db // bs,),
        in_specs=[
            pl.BlockSpec((bs, 7 * bw), lambda i: (i, 0)),
            pl.BlockSpec((bs, N_GATES), lambda i: (i, 0)),
            pl.BlockSpec((1, bs, nmh, dh, dh), lambda i: (layer, i, 0, 0, 0)),
            pl.BlockSpec((1, bs, bw), lambda i: (layer, i, 0)),
            pl.BlockSpec((1, bs, nmh), lambda i: (layer, i, 0)),
            pl.BlockSpec((1, bw), lambda i: (0, 0)),
        ],
        out_specs=[
            pl.BlockSpec((bs, bw), lambda i: (i, 0)),
            pl.BlockSpec((bs, nmh, dh, dh), lambda i: (i, 0, 0, 0)),
            pl.BlockSpec((bs, bw), lambda i: (i, 0)),
            pl.BlockSpec((bs, nmh), lambda i: (i, 0)),
        ],
        out_shape=[
            jax.ShapeDtypeStruct((db, bw), F32),
            jax.ShapeDtypeStruct((db, nmh, dh, dh), F32),
            jax.ShapeDtypeStruct((db, bw), F32),
            jax.ShapeDtypeStruct((db, nmh), F32),
        ],
        compiler_params=_cparams(("parallel",)),
    )(proj, g, state_c, state_n2, state_m, gain)


def _pick_tile(n, prefs):
    for t in prefs:
        if n % t == 0:
            return t
    return n


def kernel(x_prompt, x_sample, cache_k, cache_v, state_C, state_n, state_m, page_table, norm_ffn1, ffn1_w_in, ffn1_w_out, norm_mix, w_mix_in, b_gates, lam_q1, lam_k1, lam_q2, lam_k2, attn_subln, mlstm_gain, w_mix_out, norm_ffn2, ffn2_w_in, ffn2_w_out, norm_final):
    batch, seq, d = x_prompt.shape
    db, t_dec, _ = x_sample.shape
    depth = cache_k.shape[0]
    page = cache_k.shape[2]
    nh = cache_k.shape[3]
    past = page_table.shape[1] * page
    bw = d // 2
    nmh = N_M_HEADS
    dh = bw // nmh
    assert t_dec == 1 and nh * ATT_V_DIM == bw and w_mix_in.shape[2] == 7 * bw + N_GATES

    mp = batch * seq
    tm_p = _pick_tile(mp, (512, 256, 128))
    tm_s = db
    tf = 512
    tq = _pick_tile(seq, (512, 256, 128))
    chunk = _pick_tile(seq, (256, 128))
    bs = SUBLANES

    xp = x_prompt.reshape(mp, d)
    xs = x_sample.reshape(db, d)
    tab = _rope_table()
    gf = norm_final.reshape(1, d)
    state_n2 = state_n.reshape(depth, db, bw)

    outs_p = [[] for _ in range(5)]
    outs_s = [[] for _ in range(5)]
    for l in range(depth):
        lam_init = 0.8 - 0.6 * math.exp(-0.3 * l)
        last = l == depth - 1
        w1 = _prep_ffn_weights(ffn1_w_in[l], ffn1_w_out[l], tf)
        w2 = _prep_ffn_weights(ffn2_w_in[l], ffn2_w_out[l], tf)
        w_main = w_mix_in[l][:, :7 * bw].astype(BF16)
        wgt_hi, wgt_lo = _split_hi_lo(w_mix_in[l][:, 7 * bw:].T)
        wgt = jnp.concatenate([wgt_hi, wgt_lo], axis=0)
        bg = jnp.broadcast_to(b_gates[l].reshape(N_GATES, 1), (N_GATES, LANES))
        wo = w_mix_out[l].astype(BF16)
        wa, wh = wo[:bw], wo[bw:]
        lam = jnp.stack([lam_q1[l], lam_k1[l], lam_q2[l], lam_k2[l]])
        subln = attn_subln[l].reshape(1, ATT_V_DIM)
        gain = mlstm_gain[l].reshape(1, bw)
        g1 = norm_ffn1[l].reshape(1, d)
        gm = norm_mix[l].reshape(1, d)
        g2 = norm_ffn2[l].reshape(1, d)

        xp = _ffn(xp, g1, *w1, gf, tm=tm_p, tf=tf, final_norm=False)
        xs = _ffn(xs, g1, *w1, gf, tm=tm_s, tf=tf, final_norm=False)

        proj_p, qx_p, kvb_p, gt_p = _mixin(xp, gm, w_main, wgt, bg, tab, tm=tm_p, seq=seq, pos_offset=0, pos_step=1)
        proj_s, qx_s, kvb_s, gt_s = _mixin(xs, gm, w_main, wgt, bg, tab, tm=tm_s, seq=seq, pos_offset=past, pos_step=0)

        a_p = _attn_prompt(lam, subln, qx_p, kvb_p, batch=batch, seq=seq, tq=tq, lam_init=lam_init)
        h_p, c_p, n_p, m_p = _mlstm_prompt(proj_p, gt_p.reshape(N_GATES, 1, mp), gain, batch=batch, seq=seq, c=chunk)

        qm = qx_s.reshape(db, nh, 2, LANES).transpose(0, 2, 1, 3).reshape(db, 2 * nh, LANES)
        k_new = proj_s[:, bw:2 * bw].reshape(db, nh, ATT_V_DIM)
        v_new = proj_s[:, 2 * bw:3 * bw].reshape(db, nh, ATT_V_DIM)
        a_s = _attn_sample(page_table, lam, subln, qm, k_new, v_new, cache_k, cache_v, layer=l, lam_init=lam_init)
        h_s, c_s, n_s, m_s = _mlstm_sample(proj_s, gt_s.T, state_C, state_n2, state_m, gain, layer=l, bs=bs)

        xp = _mixout(xp, a_p, h_p, wa, wh, tm=tm_p)
        xs = _mixout(xs, a_s.reshape(db, bw), h_s, wa, wh, tm=tm_s)

        xp = _ffn(xp, g2, *w2, gf, tm=tm_p, tf=tf, final_norm=last)
        xs = _ffn(xs, g2, *w2, gf, tm=tm_s, tf=tf, final_norm=last)

        outs_p[0].append(proj_p[:, bw:2 * bw].reshape(batch, seq, nh, ATT_V_DIM))
        outs_p[1].append(proj_p[:, 2 * bw:3 * bw].reshape(batch, seq, nh, ATT_V_DIM))
        outs_p[2].append(c_p)
        outs_p[3].append(n_p.reshape(batch, nmh, dh))
        outs_p[4].append(m_p[:, :, 0, 0])
        outs_s[0].append(k_new.reshape(db, 1, nh, ATT_V_DIM))
        outs_s[1].append(v_new.reshape(db, 1, nh, ATT_V_DIM))
        outs_s[2].append(c_s)
        outs_s[3].append(n_s.reshape(db, nmh, dh))
        outs_s[4].append(m_s)

    y_prompt = xp.reshape(batch, seq, d)
    y_sample = xs.reshape(db, 1, d)
    return (y_prompt, y_sample, *[jnp.stack(o) for o in outs_p], *[jnp.stack(o) for o in outs_s])
```

```python
import functools
import math

import jax
import jax.numpy as jnp
import numpy as np
from jax import lax
from jax.experimental import pallas as pl
from jax.experimental.pallas import tpu as pltpu

F32 = jnp.float32
BF16 = jnp.bfloat16

EPS = 1e-6
ATT_HEAD_DIM = 64
ATT_V_DIM = 2 * ATT_HEAD_DIM
ATT_SCALE = ATT_HEAD_DIM ** -0.5
ROT_DIM = ATT_HEAD_DIM // 4
ROPE_THETA = 500000.0
N_M_HEADS = 4
N_GATES = 2 * N_M_HEADS
N_MIX_BLOCKS = 7

LANES = 128
SUBLANES = 8
VMEM_LIMIT_BYTES = 56 * 1024 * 1024

NEG_INF = float("-inf")


def _cparams(semantics):
    return pltpu.CompilerParams(dimension_semantics=semantics, vmem_limit_bytes=VMEM_LIMIT_BYTES)


def _rms(x):
    return x * lax.rsqrt(jnp.mean(x * x, axis=-1, keepdims=True) + EPS)


def _sigmoid(x):
    return 1.0 / (1.0 + jnp.exp(-x))


def _dot(a, b):
    return jnp.dot(a, b, preferred_element_type=F32)


def _dot_nt(a, b):
    return lax.dot_general(a, b, (((1,), (1,)), ((), ())), preferred_element_type=F32)


def _dot_tn(a, b):
    return lax.dot_general(a, b, (((0,), (0,)), ((), ())), preferred_element_type=F32)


def _split_hi_lo(x):
    hi = x.astype(BF16)
    lo = (x - hi.astype(F32)).astype(BF16)
    return hi, lo


def _any_spec():
    return pl.BlockSpec(memory_space=pl.ANY)


def _ffn_kernel(*refs, tf, rem, final_norm):
    nsub = tf // LANES
    x_ref, xs_ref, g_ref, wg_ref = refs[:4]
    wu_refs = refs[4:4 + nsub]
    wo_ref, gf_ref, o_ref, os_ref, xn_ref, xsn_ref = refs[4 + nsub:]
    i = pl.program_id(0)
    j = pl.program_id(1)
    nj = pl.num_programs(1)

    def init(src_ref, n_ref, dst_ref):
        x = src_ref[...]
        n_ref[...] = (_rms(x) * g_ref[...]).astype(BF16)
        dst_ref[...] = x

    @pl.when(j == 0)
    def _():
        init(x_ref, xn_ref, o_ref)

    @pl.when((j == 0) & (i == 0))
    def _():
        init(xs_ref, xsn_ref, os_ref)

    def step(width):
        wg = wg_ref[0, :, :width]
        wu = jnp.concatenate([r[0] for r in wu_refs[:width // LANES]], axis=1)
        wo = wo_ref[0, :width, :]

        def rows(n_ref, dst_ref):
            xn = n_ref[...]
            gate = _dot(xn, wg)
            up = _dot(xn, wu)
            act = (gate * _sigmoid(gate)) * up * 0.5
            dst_ref[...] += _dot(act.astype(BF16), wo)

        rows(xn_ref, o_ref)

        @pl.when(i == 0)
        def _():
            rows(xsn_ref, os_ref)

    if rem == tf:
        step(tf)
    else:
        @pl.when(j < nj - 1)
        def _():
            step(tf)

        @pl.when(j == nj - 1)
        def _():
            step(rem)

    if final_norm:
        @pl.when(j == nj - 1)
        def _():
            o_ref[...] = _rms(o_ref[...]) * gf_ref[...]

        @pl.when((j == nj - 1) & (i == 0))
        def _():
            os_ref[...] = _rms(os_ref[...]) * gf_ref[...]


def _ffn(x, xs, g, w_in, w_out, gf, *, layer, tm, tf, final_norm):
    m, d = x.shape
    ms = xs.shape[0]
    f = w_out.shape[1]
    assert f % LANES == 0 and tf % LANES == 0
    nj = (f + tf - 1) // tf
    rem = f - (nj - 1) * tf
    nsub = tf // LANES
    last_blk = 2 * f // LANES - 1

    def up_spec(r):
        return pl.BlockSpec((1, d, LANES),
                            lambda i, j: (layer, 0, jnp.minimum(f // LANES + nsub * j + r, last_blk)))

    return pl.pallas_call(
        functools.partial(_ffn_kernel, tf=tf, rem=rem, final_norm=final_norm),
        grid=(m // tm, nj),
        in_specs=[
            pl.BlockSpec((tm, d), lambda i, j: (i, 0)),
            pl.BlockSpec((ms, d), lambda i, j: (0, 0)),
            pl.BlockSpec((1, d), lambda i, j: (0, 0)),
            pl.BlockSpec((1, d, tf), lambda i, j: (layer, 0, j)),
        ] + [up_spec(r) for r in range(nsub)] + [
            pl.BlockSpec((1, tf, d), lambda i, j: (layer, j, 0)),
            pl.BlockSpec((1, d), lambda i, j: (0, 0)),
        ],
        out_specs=[
            pl.BlockSpec((tm, d), lambda i, j: (i, 0)),
            pl.BlockSpec((ms, d), lambda i, j: (0, 0)),
        ],
        out_shape=[jax.ShapeDtypeStruct((m, d), F32), jax.ShapeDtypeStruct((ms, d), F32)],
        scratch_shapes=[pltpu.VMEM((tm, d), BF16), pltpu.VMEM((ms, d), BF16)],
        compiler_params=_cparams(("arbitrary", "arbitrary")),
    )(x, xs, g, w_in, *([w_in] * nsub), w_out, gf)


def _rope_table():
    half = ROT_DIM // 2
    inv = ROPE_THETA ** (-jnp.arange(0, ROT_DIM, 2, dtype=F32) / ROT_DIM)
    r = np.arange(LANES) % ATT_HEAD_DIM
    inv_l = jnp.where(r < ROT_DIM, inv[r % half], 0.0)
    tab = jnp.zeros((SUBLANES, LANES), F32)
    tab = tab.at[0].set(inv_l)
    tab = tab.at[1].set(jnp.asarray(np.where(r < half, -1.0, 0.0), F32))
    tab = tab.at[2].set(jnp.asarray(np.where((r >= half) & (r < ROT_DIM), 1.0, 0.0), F32))
    return tab


def _mixin_kernel(*refs, tm, bw, seq, pos_offset, pos_step, aliased):
    x_ref, g_ref, w_ref, wgt_ref, bg_ref, tab_ref = refs[:6]
    outs = refs[6 + (2 if aliased else 0):]
    proj_ref, qx_ref, kvb_ref, gt_ref, kf_ref, vf_ref, xn_ref, cs_ref = outs
    i = pl.program_id(0)
    j = pl.program_id(1)
    half = ROT_DIM // 2
    ncol = bw // LANES

    @pl.when(j == 0)
    def _():
        xn = _rms(x_ref[...]) * g_ref[...]
        x_hi, x_lo = _split_hi_lo(xn)
        xn_ref[...] = x_hi
        row0 = ((i * tm) % seq) * pos_step + pos_offset
        pos = (row0 + pos_step * lax.broadcasted_iota(jnp.int32, (tm, LANES), 0)).astype(F32)
        ang = pos * tab_ref[0:1, :]
        sin = jnp.sin(ang)
        cs_ref[0] = jnp.cos(ang)
        cs_ref[1] = sin * tab_ref[1:2, :]
        cs_ref[2] = sin * tab_ref[2:3, :]
        wst = wgt_ref[...]
        r1 = _dot_nt(wst, x_hi)
        r2 = _dot_nt(wst, x_lo)
        pre = r1[:N_GATES] + r1[N_GATES:] + r2[:N_GATES] + bg_ref[...][:, :1]
        logsig = jnp.minimum(pre, 0.0) - jnp.log1p(jnp.exp(-jnp.abs(pre)))
        is_f = lax.broadcasted_iota(jnp.int32, pre.shape, 0) >= N_M_HEADS
        gt_ref[...] = jnp.where(is_f, logsig, pre)

    y = _dot(xn_ref[...], w_ref[0])

    def rotate(y):
        cols = []
        for c in range(ncol):
            yc = y[:, c * LANES:(c + 1) * LANES]
            cols.append(yc * cs_ref[0]
                        + pltpu.roll(yc, LANES - half, 1) * cs_ref[1]
                        + pltpu.roll(yc, half, 1) * cs_ref[2])
        return cols

    @pl.when(j == 0)
    def _():
        cols = rotate(y)
        lane = lax.broadcasted_iota(jnp.int32, (tm, LANES), 1)
        first = lane < ATT_HEAD_DIM
        for c, yc in enumerate(cols):
            yc = yc * ATT_SCALE
            qx_ref[:, (2 * c) * LANES:(2 * c + 1) * LANES] = jnp.where(first, yc, 0.0).astype(BF16)
            qx_ref[:, (2 * c + 1) * LANES:(2 * c + 2) * LANES] = jnp.where(first, 0.0, yc).astype(BF16)

    @pl.when(j == 1)
    def _():
        cols = rotate(y)
        for c, yc in enumerate(cols):
            kf_ref[0, :, c, :] = yc
            kvb_ref[:, c * LANES:(c + 1) * LANES] = yc.astype(BF16)

    @pl.when(j == 2)
    def _():
        for c in range(ncol):
            vf_ref[0, :, c, :] = y[:, c * LANES:(c + 1) * LANES]
        kvb_ref[...] = y.astype(BF16)

    @pl.when(j > 2)
    def _():
        proj_ref[...] = y


def _mixin(x, g, w_all, wgt, bg, tab, kf_prev, vf_prev, *, layer, depth, tm, seq, pos_offset, pos_step):
    m, d = x.shape
    bw = d // 2
    nh = bw // ATT_V_DIM
    aliased = kf_prev is not None
    kern = functools.partial(_mixin_kernel, tm=tm, bw=bw, seq=seq, pos_offset=pos_offset,
                             pos_step=pos_step, aliased=aliased)
    in_specs = [
        pl.BlockSpec((tm, d), lambda i, j: (i, 0)),
        pl.BlockSpec((1, d), lambda i, j: (0, 0)),
        pl.BlockSpec((1, d, bw), lambda i, j: (layer, 0, j)),
        pl.BlockSpec((2 * N_GATES, d), lambda i, j: (0, 0)),
        pl.BlockSpec((N_GATES, LANES), lambda i, j: (0, 0)),
        pl.BlockSpec((SUBLANES, LANES), lambda i, j: (0, 0)),
    ]
    args = [x, g, w_all, wgt, bg, tab]
    aliases = {}
    if aliased:
        in_specs += [_any_spec(), _any_spec()]
        args += [kf_prev, vf_prev]
        aliases = {6: 4, 7: 5}
    return pl.pallas_call(
        kern,
        grid=(m // tm, N_MIX_BLOCKS),
        in_specs=in_specs,
        out_specs=[
            pl.BlockSpec((tm, bw), lambda i, j: (i, jnp.clip(j - 3, 0, 3))),
            pl.BlockSpec((tm, 2 * bw), lambda i, j: (i, 0)),
            pl.BlockSpec((tm, bw), lambda i, j: (i, jnp.clip(j - 1, 0, 1))),
            pl.BlockSpec((N_GATES, tm), lambda i, j: (0, i)),
            pl.BlockSpec((1, tm, nh, ATT_V_DIM), lambda i, j: (layer, i, 0, 0)),
            pl.BlockSpec((1, tm, nh, ATT_V_DIM), lambda i, j: (layer, i, 0, 0)),
        ],
        out_shape=[
            jax.ShapeDtypeStruct((m, 4 * bw), F32),
            jax.ShapeDtypeStruct((m, 2 * bw), BF16),
            jax.ShapeDtypeStruct((m, 2 * bw), BF16),
            jax.ShapeDtypeStruct((N_GATES, m), F32),
            jax.ShapeDtypeStruct((depth, m, nh, ATT_V_DIM), F32),
            jax.ShapeDtypeStruct((depth, m, nh, ATT_V_DIM), F32),
        ],
        scratch_shapes=[pltpu.VMEM((tm, d), BF16), pltpu.VMEM((3, tm, LANES), F32)],
        input_output_aliases=aliases,
        compiler_params=_cparams(("arbitrary", "arbitrary")),
    )(*args)


def _mixout_kernel(x_ref, a_ref, h_ref, wa_ref, wh_ref, o_ref):
    o_ref[...] = (x_ref[...] + _dot(a_ref[...].astype(BF16), wa_ref[0])
                  + _dot(h_ref[...].astype(BF16), wh_ref[0]))


def _mixout(x, a, h, w_all, *, layer, tm):
    m, d = x.shape
    bw = a.shape[1]
    return pl.pallas_call(
        _mixout_kernel,
        grid=(m // tm,),
        in_specs=[
            pl.BlockSpec((tm, d), lambda i: (i, 0)),
            pl.BlockSpec((tm, bw), lambda i: (i, 0)),
            pl.BlockSpec((tm, bw), lambda i: (i, 0)),
            pl.BlockSpec((1, bw, d), lambda i: (layer, 0, 0)),
            pl.BlockSpec((1, bw, d), lambda i: (layer, 1, 0)),
        ],
        out_specs=pl.BlockSpec((tm, d), lambda i: (i, 0)),
        out_shape=jax.ShapeDtypeStruct((m, d), F32),
        compiler_params=_cparams(("parallel",)),
    )(x, a, h, w_all, w_all)


def _lam_from(lam_ref, lam_init):
    lv = lam_ref[...]
    s1 = jnp.sum(lv[0:1] * lv[1:2], axis=-1, keepdims=True)
    s2 = jnp.sum(lv[2:3] * lv[3:4], axis=-1, keepdims=True)
    return jnp.exp(s1) - jnp.exp(s2) + lam_init


def _attn_prompt_kernel(lam_ref, subln_ref, q_ref, k_ref, v_ref, o_ref, m_scr, acc_scr, *, tq, hp, lam_init):
    qi = pl.program_id(2)
    tk = tq
    m_scr[...] = jnp.full(m_scr.shape, NEG_INF, F32)
    acc_scr[...] = jnp.zeros(acc_scr.shape, F32)
    ones = jnp.ones((tk, LANES), BF16)
    qs = []
    for hh in range(hp):
        q2 = q_ref[:, hh * 2 * LANES:(hh + 1) * 2 * LANES]
        qs.append(jnp.concatenate([q2[:, :LANES], q2[:, LANES:]], axis=0))

    def chunk(j, masked):
        start = pl.multiple_of(j * tk, tk)
        for hh in range(hp):
            kc = k_ref[pl.ds(start, tk), hh * LANES:(hh + 1) * LANES]
            vc = v_ref[pl.ds(start, tk), hh * LANES:(hh + 1) * LANES]
            s = _dot_nt(qs[hh], kc)
            if masked:
                row = lax.broadcasted_iota(jnp.int32, s.shape, 0)
                row = jnp.where(row >= tq, row - tq, row)
                col = lax.broadcasted_iota(jnp.int32, s.shape, 1)
                s = jnp.where(col <= row, s, NEG_INF)
            m_prev = m_scr[hh]
            m_new = jnp.maximum(m_prev, jnp.max(s, axis=1, keepdims=True))
            alpha = jnp.exp(m_prev - m_new)
            p = jnp.exp(s - pltpu.repeat(m_new, tk // LANES, 1))
            vx = jnp.concatenate([vc, ones], axis=1)
            acc_scr[hh] = pltpu.repeat(alpha, 2, 1) * acc_scr[hh] + _dot(p.astype(BF16), vx)
            m_scr[hh] = m_new

    def body(j, carry):
        chunk(j, False)
        return carry

    lax.fori_loop(0, qi, body, 0)
    chunk(qi, True)

    lam = _lam_from(lam_ref, lam_init)
    for hh in range(hp):
        acc = acc_scr[hh]
        o1 = acc[:tq, :LANES] / acc[:tq, LANES:]
        o2 = acc[tq:, :LANES] / acc[tq:, LANES:]
        o = o1 - lam * o2
        o_ref[:, hh * LANES:(hh + 1) * LANES] = (_rms(o) * subln_ref[...] * (1.0 - lam_init)).astype(o_ref.dtype)


def _attn_prompt(lam, subln, qx, kvb, *, batch, seq, tq, hp, lam_init):
    m = qx.shape[0]
    bw = kvb.shape[1] // 2
    nh = bw // ATT_V_DIM
    ng = nh // hp
    nq = seq // tq
    kern = functools.partial(_attn_prompt_kernel, tq=tq, hp=hp, lam_init=lam_init)
    return pl.pallas_call(
        kern,
        grid=(batch, ng, nq),
        in_specs=[
            pl.BlockSpec(lam.shape, lambda b, h, i: (0, 0)),
            pl.BlockSpec((1, ATT_V_DIM), lambda b, h, i: (0, 0)),
            pl.BlockSpec((tq, hp * 2 * LANES), lambda b, h, i: (b * nq + i, h)),
            pl.BlockSpec((seq, hp * LANES), lambda b, h, i: (b, h)),
            pl.BlockSpec((seq, hp * LANES), lambda b, h, i: (b, ng + h)),
        ],
        out_specs=pl.BlockSpec((tq, hp * ATT_V_DIM), lambda b, h, i: (b * nq + i, h)),
        out_shape=jax.ShapeDtypeStruct((m, bw), BF16),
        scratch_shapes=[pltpu.VMEM((hp, 2 * tq, LANES), F32), pltpu.VMEM((hp, 2 * tq, 2 * LANES), F32)],
        compiler_params=_cparams(("parallel", "parallel", "arbitrary")),
    )(lam, subln, qx, kvb, kvb)


def _attn_sample_kernel(pt_ref, lam_ref, subln_ref, qm_ref, kn_ref, vn_ref, *rest,
                        n_pages, page, nh, lam_init):
    k_refs = rest[:n_pages]
    v_refs = rest[n_pages:2 * n_pages]
    o_ref = rest[2 * n_pages]
    s_scr = rest[2 * n_pages + 1]
    rows = page * nh
    qm = qm_ref[0]

    def head_mask(width):
        lane = lax.broadcasted_iota(jnp.int32, (nh, width), 1)
        sub = lax.broadcasted_iota(jnp.int32, (nh, width), 0)
        return (lane % nh) == sub

    def scores(kflat, width):
        st = _dot_nt(qm, kflat)
        st = st.reshape(2, nh, width)
        return jnp.sum(jnp.where(head_mask(width)[None], st, 0.0), axis=1)

    for j in range(n_pages):
        kf = k_refs[j][0, 0].reshape(rows, LANES).astype(BF16)
        s_scr[0:2, j * rows:(j + 1) * rows] = scores(kf, rows)
    pad = jnp.zeros((LANES - nh, LANES), F32)
    knp = jnp.concatenate([kn_ref[0, 0], pad], axis=0).astype(BF16)
    s_new = scores(knp, LANES)
    lane = lax.broadcasted_iota(jnp.int32, s_new.shape, 1)
    s_scr[0:2, n_pages * rows:] = jnp.where(lane < nh, s_new, NEG_INF)

    s_all = s_scr[0:2, :]
    ncol = s_all.shape[1] // LANES

    def per_head_allreduce(x, op):
        r = x[:, :LANES]
        for c in range(1, ncol):
            r = op(r, x[:, c * LANES:(c + 1) * LANES])
        sh = nh
        while sh < LANES:
            r = op(r, pltpu.roll(r, sh, 1))
            sh *= 2
        return r

    mx = per_head_allreduce(s_all, jnp.maximum)
    p = jnp.exp(s_all - pltpu.repeat(mx, ncol, 1))
    den = per_head_allreduce(p, jnp.add)
    pn = p / pltpu.repeat(den, ncol, 1)
    w = pn[0:1] - _lam_from(lam_ref, lam_init) * pn[1:2]

    def weighted(wj, vflat, width):
        a = jnp.where(head_mask(width), jnp.broadcast_to(wj, (nh, width)), 0.0)
        return _dot(a.astype(BF16), vflat)

    o = jnp.zeros((nh, LANES), F32)
    for j in range(n_pages):
        vf = v_refs[j][0, 0].reshape(rows, LANES).astype(BF16)
        o = o + weighted(w[:, j * rows:(j + 1) * rows], vf, rows)
    vnp = jnp.concatenate([vn_ref[0, 0], pad], axis=0).astype(BF16)
    o = o + weighted(w[:, n_pages * rows:], vnp, LANES)
    o_ref[0] = _rms(o) * subln_ref[...] * (1.0 - lam_init)


def _attn_sample(page_table, lam, subln, qm, kf_all, vf_all, cache_k, cache_v, *, layer, lam_init):
    db, n_pages = page_table.shape
    _, _, page, nh, vd = cache_k.shape
    rows = page * nh

    def page_spec(j):
        return pl.BlockSpec((1, 1, page, nh, vd), lambda b, pt: (layer, pt[b, j], 0, 0, 0))

    kern = functools.partial(_attn_sample_kernel, n_pages=n_pages, page=page, nh=nh, lam_init=lam_init)
    grid_spec = pltpu.PrefetchScalarGridSpec(
        num_scalar_prefetch=1,
        grid=(db,),
        in_specs=[
            pl.BlockSpec(lam.shape, lambda b, pt: (0, 0)),
            pl.BlockSpec((1, vd), lambda b, pt: (0, 0)),
            pl.BlockSpec((1, 2 * nh, LANES), lambda b, pt: (b, 0, 0)),
            pl.BlockSpec((1, 1, nh, vd), lambda b, pt: (layer, b, 0, 0)),
            pl.BlockSpec((1, 1, nh, vd), lambda b, pt: (layer, b, 0, 0)),
        ] + [page_spec(j) for j in range(n_pages)] + [page_spec(j) for j in range(n_pages)],
        out_specs=pl.BlockSpec((1, nh, vd), lambda b, pt: (b, 0, 0)),
        scratch_shapes=[pltpu.VMEM((SUBLANES, n_pages * rows + LANES), F32)],
    )
    return pl.pallas_call(
        kern,
        grid_spec=grid_spec,
        out_shape=jax.ShapeDtypeStruct((db, nh, vd), F32),
        compiler_params=_cparams(("arbitrary",)),
    )(page_table, lam, subln, qm, kf_all, vf_all, *([cache_k] * n_pages), *([cache_v] * n_pages))


def _mlstm_prompt_kernel(q_ref, k_ref, v_ref, mo_ref, gt_ref, gain_ref,
                         h_ref, c_out_ref, n_out_ref, m_out_ref,
                         c_scr, n_scr, m_scr, *, c, dh):
    nmh = N_M_HEADS
    kk = pl.program_id(1)

    @pl.when(kk == 0)
    def _():
        c_scr[...] = jnp.zeros(c_scr.shape, F32)
        n_scr[...] = jnp.zeros(n_scr.shape, F32)
        m_scr[...] = jnp.zeros(m_scr.shape, F32)

    rows = gt_ref[...]
    r_hi, r_lo = _split_hi_lo(rows)
    ti = lax.broadcasted_iota(jnp.int32, (c, c), 0)
    si = lax.broadcasted_iota(jnp.int32, (c, c), 1)
    tril = (si <= ti)
    lower = tril.astype(BF16)
    upper = (ti <= si).astype(BF16)
    eye = (ti == si).astype(BF16)
    cum_rows = _dot(r_hi, upper) + _dot(r_lo, upper)
    cum_cols = _dot_nt(lower, r_hi) + _dot_nt(lower, r_lo)
    id_cols = _dot_nt(eye, r_hi) + _dot_nt(eye, r_lo)

    for h in range(nmh):
        hs = slice(h * dh, (h + 1) * dh)
        b_row = cum_rows[nmh + h:nmh + h + 1, :]
        ig_row = rows[h:h + 1, :]
        b_col = cum_cols[:, nmh + h:nmh + h + 1]
        ig_col = id_cols[:, h:h + 1]
        b_last = b_col[c - 1:c, :]
        m_prev = m_scr[h, 0:1, 0:1]

        d_mat = jnp.where(tril, b_col - b_row + ig_row, NEG_INF)
        inter = b_col + m_prev
        m_t = jnp.maximum(inter, jnp.max(d_mat, axis=1, keepdims=True))
        w_intra = jnp.exp(d_mat - m_t)
        w_inter = jnp.exp(inter - m_t)

        qb = q_ref[:, hs].astype(BF16)
        kf = k_ref[:, hs] * (dh ** -0.5)
        kb = kf.astype(BF16)
        vb = v_ref[:, hs].astype(BF16)
        a = _dot_nt(qb, kb) * w_intra
        c_prev = c_scr[h]
        n_prev = n_scr[h]
        num = _dot(a.astype(BF16), vb) + w_inter * _dot(qb, c_prev.astype(BF16))
        qn = jnp.sum(qb.astype(F32) * n_prev.astype(BF16).astype(F32), axis=1, keepdims=True)
        den = jnp.sum(a, axis=1, keepdims=True) + w_inter * qn
        den = jnp.maximum(jnp.abs(den), jnp.exp(-m_t))
        hh = num / den

        m_new = m_t[c - 1:c, :]
        w_s = jnp.exp(b_last - b_col + ig_col - m_new)
        decay = jnp.exp(b_last + m_prev - m_new)
        kw = kf * w_s
        c_new = decay * c_prev + _dot_tn(kw.astype(BF16), vb)
        n_new = decay * n_prev + jnp.sum(kw, axis=0, keepdims=True)
        c_scr[h] = c_new
        n_scr[h] = n_new
        m_scr[h] = jnp.broadcast_to(m_new, (SUBLANES, LANES))
        c_out_ref[0, h] = c_new
        n_out_ref[0, h] = n_new
        m_out_ref[0, h] = jnp.broadcast_to(m_new, (1, LANES))

        h_ref[:, hs] = (_rms(hh) * gain_ref[:, hs] * _sigmoid(mo_ref[:, hs])).astype(h_ref.dtype)


def _mlstm_prompt(proj, gt, gain, *, batch, seq, c):
    m = proj.shape[0]
    bw = proj.shape[1] // 4
    nmh = N_M_HEADS
    dh = bw // nmh
    nc = seq // c
    kern = functools.partial(_mlstm_prompt_kernel, c=c, dh=dh)

    def col(base):
        return lambda b, k: (b * nc + k, base)

    return pl.pallas_call(
        kern,
        grid=(batch, nc),
        in_specs=[
            pl.BlockSpec((c, bw), col(0)),
            pl.BlockSpec((c, bw), col(1)),
            pl.BlockSpec((c, bw), col(2)),
            pl.BlockSpec((c, bw), col(3)),
            pl.BlockSpec((N_GATES, c), lambda b, k: (0, b * nc + k)),
            pl.BlockSpec((1, bw), lambda b, k: (0, 0)),
        ],
        out_specs=[
            pl.BlockSpec((c, bw), lambda b, k: (b * nc + k, 0)),
            pl.BlockSpec((1, nmh, dh, dh), lambda b, k: (b, 0, 0, 0)),
            pl.BlockSpec((1, nmh, 1, dh), lambda b, k: (b, 0, 0, 0)),
            pl.BlockSpec((1, nmh, 1, LANES), lambda b, k: (b, 0, 0, 0)),
        ],
        out_shape=[
            jax.ShapeDtypeStruct((m, bw), BF16),
            jax.ShapeDtypeStruct((batch, nmh, dh, dh), F32),
            jax.ShapeDtypeStruct((batch, nmh, 1, dh), F32),
            jax.ShapeDtypeStruct((batch, nmh, 1, LANES), F32),
        ],
        scratch_shapes=[pltpu.VMEM((nmh, dh, dh), F32), pltpu.VMEM((nmh, 1, dh), F32),
                        pltpu.VMEM((nmh, SUBLANES, LANES), F32)],
        compiler_params=_cparams(("parallel", "arbitrary")),
    )(proj, proj, proj, proj, gt, gain)


def _mlstm_sample_kernel(*refs, bs, bw, dh, aliased):
    p_ref, g_ref, c_ref, n_ref, m_ref, gain_ref = refs[:6]
    h_ref, c_out_ref, n_out_ref, m_out_ref = refs[6 + (1 if aliased else 0):]
    nmh = N_M_HEADS
    g = g_ref[...]
    m_all = m_ref[0]
    rowid = lax.broadcasted_iota(jnp.int32, (bs, dh), 0)
    for h in range(nmh):
        def cols(base):
            return p_ref[:, base * bw + h * dh: base * bw + (h + 1) * dh]
        q, k, v, mo = cols(0), cols(1) * (dh ** -0.5), cols(2), cols(3)
        ig = g[:, h:h + 1]
        lf = g[:, nmh + h:nmh + h + 1]
        m_prev = m_all[:, h:h + 1]
        m_t = jnp.maximum(lf + m_prev, ig)
        w_i = jnp.exp(ig - m_t)
        w_f = jnp.exp(lf + m_prev - m_t)
        qb = q.astype(BF16)
        qf = qb.astype(F32)
        kb = k.astype(BF16).astype(F32)
        vb = v.astype(BF16)
        n_prev = n_ref[0][:, h * dh:(h + 1) * dh]
        a = jnp.sum(qf * kb, axis=1, keepdims=True) * w_i
        qn = jnp.sum(qf * n_prev.astype(BF16).astype(F32), axis=1, keepdims=True)
        kw = k * w_i
        qc = jnp.zeros((bs, dh), F32)
        for j in range(bs):
            c_prev = c_ref[0, j, h]
            qc = jnp.where(rowid == j, _dot(qb, c_prev.astype(BF16)), qc)
            kwj = jnp.where(rowid == j, kw, 0.0).astype(BF16)
            c_out_ref[0, j, h] = w_f[j:j + 1, :] * c_prev + _dot_tn(kwj, vb)
        num = a * vb.astype(F32) + w_f * qc
        den = a + w_f * qn
        den = jnp.maximum(jnp.abs(den), jnp.exp(-m_t))
        hh = num / den
        n_out_ref[:, h * dh:(h + 1) * dh] = w_f * n_prev + kw
        m_out_ref[:, h:h + 1] = m_t
        gain = gain_ref[:, h * dh:(h + 1) * dh]
        h_ref[:, h * dh:(h + 1) * dh] = _rms(hh) * gain * _sigmoid(mo)


def _mlstm_sample(proj, g, state_c, state_n2, state_m, gain, c_prev_out, *, layer, bs):
    db = proj.shape[0]
    depth = state_c.shape[0]
    bw = proj.shape[1] // 4
    nmh = N_M_HEADS
    dh = bw // nmh
    aliased = c_prev_out is not None
    kern = functools.partial(_mlstm_sample_kernel, bs=bs, bw=bw, dh=dh, aliased=aliased)
    in_specs = [
        pl.BlockSpec((bs, 4 * bw), lambda i: (i, 0)),
        pl.BlockSpec((bs, N_GATES), lambda i: (i, 0)),
        pl.BlockSpec((1, bs, nmh, dh, dh), lambda i: (layer, i, 0, 0, 0)),
        pl.BlockSpec((1, bs, bw), lambda i: (layer, i, 0)),
        pl.BlockSpec((1, bs, nmh), lambda i: (layer, i, 0)),
        pl.BlockSpec((1, bw), lambda i: (0, 0)),
    ]
    args = [proj, g, state_c, state_n2, state_m, gain]
    aliases = {}
    if aliased:
        in_specs.append(_any_spec())
        args.append(c_prev_out)
        aliases = {6: 1}
    return pl.pallas_call(
        kern,
        grid=(db // bs,),
        in_specs=in_specs,
        out_specs=[
            pl.BlockSpec((bs, bw), lambda i: (i, 0)),
            pl.BlockSpec((1, bs, nmh, dh, dh), lambda i: (layer, i, 0, 0, 0)),
            pl.BlockSpec((bs, bw), lambda i: (i, 0)),
            pl.BlockSpec((bs, nmh), lambda i: (i, 0)),
        ],
        out_shape=[
            jax.ShapeDtypeStruct((db, bw), F32),
            jax.ShapeDtypeStruct((depth, db, nmh, dh, dh), F32),
            jax.ShapeDtypeStruct((db, bw), F32),
            jax.ShapeDtypeStruct((db, nmh), F32),
        ],
        input_output_aliases=aliases,
        compiler_params=_cparams(("parallel",)),
    )(*args)


def _pick_tile(n, prefs):
    for t in prefs:
        if n % t == 0:
            return t
    return n


def kernel(x_prompt, x_sample, cache_k, cache_v, state_C, state_n, state_m, page_table, norm_ffn1, ffn1_w_in, ffn1_w_out, norm_mix, w_mix_in, b_gates, lam_q1, lam_k1, lam_q2, lam_k2, attn_subln, mlstm_gain, w_mix_out, norm_ffn2, ffn2_w_in, ffn2_w_out, norm_final):
    batch, seq, d = x_prompt.shape
    db, t_dec, _ = x_sample.shape
    depth = cache_k.shape[0]
    page = cache_k.shape[2]
    nh = cache_k.shape[3]
    past = page_table.shape[1] * page
    bw = d // 2
    nmh = N_M_HEADS
    dh = bw // nmh
    nmain = N_MIX_BLOCKS * bw
    assert t_dec == 1 and nh * ATT_V_DIM == bw and w_mix_in.shape[2] == nmain + N_GATES

    mp = batch * seq
    tm_p = _pick_tile(mp, (512, 256, 128))
    tm_s = db
    tf = 512
    tq = _pick_tile(seq, (512, 256, 128))
    hp = 4 if nh % 4 == 0 else 1
    chunk = _pick_tile(seq, (256, 128))
    bs = SUBLANES

    xp = x_prompt.reshape(mp, d)
    xs = x_sample.reshape(db, d)
    tab = _rope_table()
    gf = norm_final.reshape(1, d)
    state_n2 = state_n.reshape(depth, db, bw)

    w1_in, w1_out = ffn1_w_in.astype(BF16), ffn1_w_out.astype(BF16)
    w2_in, w2_out = ffn2_w_in.astype(BF16), ffn2_w_out.astype(BF16)
    wmix = w_mix_in.astype(BF16)
    wout = w_mix_out.astype(BF16)
    wg_small = lax.optimization_barrier(w_mix_in[:, :, nmain:])

    kf_p = vf_p = kf_s = vf_s = c_s_all = None
    outs_p = [[] for _ in range(3)]
    outs_s = [[] for _ in range(2)]
    for l in range(depth):
        lam_init = 0.8 - 0.6 * math.exp(-0.3 * l)
        last = l == depth - 1
        wgt_hi, wgt_lo = _split_hi_lo(wg_small[l].T)
        wgt = jnp.concatenate([wgt_hi, wgt_lo], axis=0)
        bg = jnp.broadcast_to(b_gates[l].reshape(N_GATES, 1), (N_GATES, LANES))
        lam = jnp.stack([lam_q1[l], lam_k1[l], lam_q2[l], lam_k2[l]])
        subln = attn_subln[l].reshape(1, ATT_V_DIM)
        gain = mlstm_gain[l].reshape(1, bw)
        g1 = norm_ffn1[l].reshape(1, d)
        gm = norm_mix[l].reshape(1, d)
        g2 = norm_ffn2[l].reshape(1, d)

        xp, xs = _ffn(xp, xs, g1, w1_in, w1_out, gf, layer=l, tm=tm_p, tf=tf, final_norm=False)

        proj_p, qx_p, kvb_p, gt_p, kf_p, vf_p = _mixin(
            xp, gm, wmix, wgt, bg, tab, kf_p, vf_p, layer=l, depth=depth, tm=tm_p, seq=seq, pos_offset=0, pos_step=1)
        proj_s, qx_s, kvb_s, gt_s, kf_s, vf_s = _mixin(
            xs, gm, wmix, wgt, bg, tab, kf_s, vf_s, layer=l, depth=depth, tm=tm_s, seq=seq, pos_offset=past, pos_step=0)

        a_p = _attn_prompt(lam, subln, qx_p, kvb_p, batch=batch, seq=seq, tq=tq, hp=hp, lam_init=lam_init)
        h_p, c_p, n_p, m_p = _mlstm_prompt(proj_p, gt_p, gain, batch=batch, seq=seq, c=chunk)

        qm = qx_s.reshape(db, nh, 2, LANES).transpose(0, 2, 1, 3).reshape(db, 2 * nh, LANES)
        a_s = _attn_sample(page_table, lam, subln, qm, kf_s, vf_s, cache_k, cache_v, layer=l, lam_init=lam_init)
        h_s, c_s_all, n_s, m_s = _mlstm_sample(proj_s, gt_s.T, state_C, state_n2, state_m, gain, c_s_all, layer=l, bs=bs)

        xp = _mixout(xp, a_p, h_p, wout, layer=l, tm=tm_p)
        xs = _mixout(xs, a_s.reshape(db, bw), h_s, wout, layer=l, tm=tm_s)

        xp, xs = _ffn(xp, xs, g2, w2_in, w2_out, gf, layer=l, tm=tm_p, tf=tf, final_norm=last)

        outs_p[0].append(c_p)
        outs_p[1].append(n_p.reshape(batch, nmh, dh))
        outs_p[2].append(m_p[:, :, 0, 0])
        outs_s[0].append(n_s.reshape(db, nmh, dh))
        outs_s[1].append(m_s)

    y_prompt = xp.reshape(batch, seq, d)
    y_sample = xs.reshape(db, 1, d)
    return (y_prompt, y_sample,
            kf_p.reshape(depth, batch, seq, nh, ATT_V_DIM), vf_p.reshape(depth, batch, seq, nh, ATT_V_DIM),
            *[jnp.stack(o) for o in outs_p],
            kf_s.reshape(depth, db, 1, nh, ATT_V_DIM), vf_s.reshape(depth, db, 1, nh, ATT_V_DIM),
            c_s_all, *[jnp.stack(o) for o in outs_s])
```

```python
import functools
import math

import jax
import jax.numpy as jnp
import numpy as np
from jax import lax
from jax.experimental import pallas as pl
from jax.experimental.pallas import tpu as pltpu

F32 = jnp.float32
BF16 = jnp.bfloat16

EPS = 1e-6
ATT_HEAD_DIM = 64
ATT_V_DIM = 2 * ATT_HEAD_DIM
ATT_SCALE = ATT_HEAD_DIM ** -0.5
ROT_DIM = ATT_HEAD_DIM // 4
ROPE_THETA = 500000.0
N_M_HEADS = 4
N_GATES = 2 * N_M_HEADS
N_MIX_BLOCKS = 7

LANES = 128
SUBLANES = 8
VMEM_LIMIT_BYTES = 56 * 1024 * 1024

NEG_INF = float("-inf")


def _cparams(semantics):
    return pltpu.CompilerParams(dimension_semantics=semantics, vmem_limit_bytes=VMEM_LIMIT_BYTES)


def _rms(x):
    return x * lax.rsqrt(jnp.mean(x * x, axis=-1, keepdims=True) + EPS)


def _sigmoid(x):
    return 1.0 / (1.0 + jnp.exp(-x))


def _dot(a, b):
    return jnp.dot(a, b, preferred_element_type=F32)


def _dot_nt(a, b):
    return lax.dot_general(a, b, (((1,), (1,)), ((), ())), preferred_element_type=F32)


def _dot_tn(a, b):
    return lax.dot_general(a, b, (((0,), (0,)), ((), ())), preferred_element_type=F32)


def _split_hi_lo(x):
    hi = x.astype(BF16)
    lo = (x - hi.astype(F32)).astype(BF16)
    return hi, lo


def _lane_tile(x, n):
    return jnp.concatenate([x] * n, axis=1)


def _any_spec():
    return pl.BlockSpec(memory_space=pl.ANY)


def _ffn_kernel(*refs, tf, rem, final_norm):
    nsub = tf // LANES
    x_ref, xs_ref, g_ref, wg_ref = refs[:4]
    wu_refs = refs[4:4 + nsub]
    wo_ref, gf_ref, o_ref, os_ref, xn_ref, xsn_ref = refs[4 + nsub:]
    i = pl.program_id(0)
    j = pl.program_id(1)
    nj = pl.num_programs(1)

    def init(src_ref, n_ref, dst_ref):
        x = src_ref[...]
        n_ref[...] = (_rms(x) * g_ref[...]).astype(BF16)
        dst_ref[...] = x

    @pl.when(j == 0)
    def _():
        init(x_ref, xn_ref, o_ref)

    @pl.when((j == 0) & (i == 0))
    def _():
        init(xs_ref, xsn_ref, os_ref)

    def step(width):
        wg = wg_ref[0, :, :width].astype(BF16)
        wu = jnp.concatenate([r[0].astype(BF16) for r in wu_refs[:width // LANES]], axis=1)
        wo = wo_ref[0, :width, :].astype(BF16)

        def rows(n_ref, dst_ref):
            xn = n_ref[...]
            gate = _dot(xn, wg)
            up = _dot(xn, wu)
            act = (gate * _sigmoid(gate)) * up * 0.5
            dst_ref[...] += _dot(act.astype(BF16), wo)

        rows(xn_ref, o_ref)

        @pl.when(i == 0)
        def _():
            rows(xsn_ref, os_ref)

    if rem == tf:
        step(tf)
    else:
        @pl.when(j < nj - 1)
        def _():
            step(tf)

        @pl.when(j == nj - 1)
        def _():
            step(rem)

    if final_norm:
        @pl.when(j == nj - 1)
        def _():
            o_ref[...] = _rms(o_ref[...]) * gf_ref[...]

        @pl.when((j == nj - 1) & (i == 0))
        def _():
            os_ref[...] = _rms(os_ref[...]) * gf_ref[...]


def _ffn(x, xs, g, w_in, w_out, gf, *, layer, tm, tf, final_norm):
    m, d = x.shape
    ms = xs.shape[0]
    f = w_out.shape[1]
    assert f % LANES == 0 and tf % LANES == 0
    nj = (f + tf - 1) // tf
    rem = f - (nj - 1) * tf
    nsub = tf // LANES
    last_blk = 2 * f // LANES - 1

    def up_spec(r):
        return pl.BlockSpec((1, d, LANES),
                            lambda i, j: (layer, 0, jnp.minimum(f // LANES + nsub * j + r, last_blk)))

    return pl.pallas_call(
        functools.partial(_ffn_kernel, tf=tf, rem=rem, final_norm=final_norm),
        grid=(m // tm, nj),
        in_specs=[
            pl.BlockSpec((tm, d), lambda i, j: (i, 0)),
            pl.BlockSpec((ms, d), lambda i, j: (0, 0)),
            pl.BlockSpec((1, d), lambda i, j: (0, 0)),
            pl.BlockSpec((1, d, tf), lambda i, j: (layer, 0, j)),
        ] + [up_spec(r) for r in range(nsub)] + [
            pl.BlockSpec((1, tf, d), lambda i, j: (layer, j, 0)),
            pl.BlockSpec((1, d), lambda i, j: (0, 0)),
        ],
        out_specs=[
            pl.BlockSpec((tm, d), lambda i, j: (i, 0)),
            pl.BlockSpec((ms, d), lambda i, j: (0, 0)),
        ],
        out_shape=[jax.ShapeDtypeStruct((m, d), F32), jax.ShapeDtypeStruct((ms, d), F32)],
        scratch_shapes=[pltpu.VMEM((tm, d), BF16), pltpu.VMEM((ms, d), BF16)],
        compiler_params=_cparams(("arbitrary", "arbitrary")),
    )(x, xs, g, w_in, *([w_in] * nsub), w_out, gf)


def _rope_table():
    half = ROT_DIM // 2
    inv = ROPE_THETA ** (-jnp.arange(0, ROT_DIM, 2, dtype=F32) / ROT_DIM)
    r = np.arange(LANES) % ATT_HEAD_DIM
    inv_l = jnp.where(r < ROT_DIM, inv[r % half], 0.0)
    tab = jnp.zeros((SUBLANES, LANES), F32)
    tab = tab.at[0].set(inv_l)
    tab = tab.at[1].set(jnp.asarray(np.where(r < half, -1.0, 0.0), F32))
    tab = tab.at[2].set(jnp.asarray(np.where((r >= half) & (r < ROT_DIM), 1.0, 0.0), F32))
    return tab


def _mixin_kernel(*refs, tm, bw, seq, pos_offset, pos_step, aliased):
    x_ref, g_ref, w_ref, wgt_ref, bg_ref, tab_ref = refs[:6]
    outs = refs[6 + (2 if aliased else 0):]
    proj_ref, qx_ref, kvb_ref, gt_ref, kf_ref, vf_ref, xn_ref, cs_ref, base_ref, raw_a, raw_b = outs
    assert N_MIX_BLOCKS == 7
    i = pl.program_id(0)
    j = pl.program_id(1)
    half = ROT_DIM // 2
    ncol = bw // LANES

    def matmul():
        return _dot(xn_ref[...], w_ref[0])

    def put_raw(raw_ref, y):
        for c in range(ncol):
            raw_ref[c * tm:(c + 1) * tm, :] = y[:, c * LANES:(c + 1) * LANES]

    def rotate(raw_ref):
        cols = []
        for c in range(ncol):
            yc = raw_ref[c * tm:(c + 1) * tm, :]
            cols.append(yc * cs_ref[0]
                        + pltpu.roll(yc, LANES - half, 1) * cs_ref[1]
                        + pltpu.roll(yc, half, 1) * cs_ref[2])
        return cols

    @pl.when((i == 0) & (j == 0))
    def _():
        off = (pos_step * lax.broadcasted_iota(jnp.int32, (tm, LANES), 0)).astype(F32)
        ang_b = off * tab_ref[0:1, :]
        base_ref[0] = jnp.cos(ang_b)
        base_ref[1] = jnp.sin(ang_b)

    @pl.when(j == 0)
    def _():
        xn = _rms(x_ref[...]) * g_ref[...]
        x_hi, x_lo = _split_hi_lo(xn)
        xn_ref[...] = x_hi
        put_raw(raw_a, _dot(x_hi, w_ref[0]))
        row0 = ((i * tm) % seq) * pos_step + pos_offset
        ang_a = row0.astype(F32) * tab_ref[0:1, :]
        cos_a = jnp.cos(ang_a)
        sin_a = jnp.sin(ang_a)
        cos = cos_a * base_ref[0] - sin_a * base_ref[1]
        sin = sin_a * base_ref[0] + cos_a * base_ref[1]
        cs_ref[0] = cos
        cs_ref[1] = sin * tab_ref[1:2, :]
        cs_ref[2] = sin * tab_ref[2:3, :]
        wst = wgt_ref[...]
        r1 = _dot_nt(wst, x_hi)
        r2 = _dot_nt(wst, x_lo)
        pre = r1[:N_GATES] + r1[N_GATES:] + r2[:N_GATES] + bg_ref[...][:, :1]
        logsig = jnp.minimum(pre, 0.0) - jnp.log1p(jnp.exp(-jnp.abs(pre)))
        is_f = lax.broadcasted_iota(jnp.int32, pre.shape, 0) >= N_M_HEADS
        gt_ref[...] = jnp.where(is_f, logsig, pre)

    @pl.when(j == 1)
    def _():
        put_raw(raw_b, matmul())
        cols = rotate(raw_a)
        lane = lax.broadcasted_iota(jnp.int32, (tm, LANES), 1)
        first = lane < ATT_HEAD_DIM
        for c, yc in enumerate(cols):
            yc = yc * ATT_SCALE
            qx_ref[:, (2 * c) * LANES:(2 * c + 1) * LANES] = jnp.where(first, yc, 0.0).astype(BF16)
            qx_ref[:, (2 * c + 1) * LANES:(2 * c + 2) * LANES] = jnp.where(first, 0.0, yc).astype(BF16)

    @pl.when(j == 2)
    def _():
        put_raw(raw_a, matmul())
        cols = rotate(raw_b)
        for c, yc in enumerate(cols):
            raw_b[c * tm:(c + 1) * tm, :] = yc
            kvb_ref[:, c * LANES:(c + 1) * LANES] = yc.astype(BF16)

    def gather_tokens(dst_ref, raw_ref, part):
        lo, hi = (0, tm // 2) if part == 0 else (tm // 2, tm)
        blk = jnp.stack([raw_ref[c * tm + lo:c * tm + hi, :] for c in range(ncol)], axis=0)
        dst_ref[0, lo:hi] = jnp.swapaxes(blk, 0, 1)

    @pl.when(j == 3)
    def _():
        proj_ref[...] = matmul()
        for c in range(ncol):
            kvb_ref[:, c * LANES:(c + 1) * LANES] = raw_a[c * tm:(c + 1) * tm, :].astype(BF16)
        gather_tokens(kf_ref, raw_b, 0)

    @pl.when(j == 4)
    def _():
        proj_ref[...] = matmul()
        gather_tokens(kf_ref, raw_b, 1)

    @pl.when(j == 5)
    def _():
        proj_ref[...] = matmul()
        gather_tokens(vf_ref, raw_a, 0)

    @pl.when(j == 6)
    def _():
        proj_ref[...] = matmul()
        gather_tokens(vf_ref, raw_a, 1)


def _mixin(x, g, w_all, wgt, bg, tab, kf_prev, vf_prev, *, layer, depth, tm, seq, pos_offset, pos_step):
    m, d = x.shape
    bw = d // 2
    nh = bw // ATT_V_DIM
    aliased = kf_prev is not None
    kern = functools.partial(_mixin_kernel, tm=tm, bw=bw, seq=seq, pos_offset=pos_offset,
                             pos_step=pos_step, aliased=aliased)
    in_specs = [
        pl.BlockSpec((tm, d), lambda i, j: (i, 0)),
        pl.BlockSpec((1, d), lambda i, j: (0, 0)),
        pl.BlockSpec((1, d, bw), lambda i, j: (layer, 0, j)),
        pl.BlockSpec((2 * N_GATES, d), lambda i, j: (0, 0)),
        pl.BlockSpec((N_GATES, LANES), lambda i, j: (0, 0)),
        pl.BlockSpec((SUBLANES, LANES), lambda i, j: (0, 0)),
    ]
    args = [x, g, w_all, wgt, bg, tab]
    aliases = {}
    if aliased:
        in_specs += [_any_spec(), _any_spec()]
        args += [kf_prev, vf_prev]
        aliases = {6: 4, 7: 5}
    return pl.pallas_call(
        kern,
        grid=(m // tm, N_MIX_BLOCKS),
        in_specs=in_specs,
        out_specs=[
            pl.BlockSpec((tm, bw), lambda i, j: (i, jnp.clip(j - 3, 0, 3))),
            pl.BlockSpec((tm, 2 * bw), lambda i, j: (i, 0)),
            pl.BlockSpec((tm, bw), lambda i, j: (i, jnp.clip(j - 2, 0, 1))),
            pl.BlockSpec((N_GATES, tm), lambda i, j: (0, i)),
            pl.BlockSpec((1, tm, nh, ATT_V_DIM), lambda i, j: (layer, i, 0, 0)),
            pl.BlockSpec((1, tm, nh, ATT_V_DIM), lambda i, j: (layer, i, 0, 0)),
        ],
        out_shape=[
            jax.ShapeDtypeStruct((m, 4 * bw), F32),
            jax.ShapeDtypeStruct((m, 2 * bw), BF16),
            jax.ShapeDtypeStruct((m, 2 * bw), BF16),
            jax.ShapeDtypeStruct((N_GATES, m), F32),
            jax.ShapeDtypeStruct((depth, m, nh, ATT_V_DIM), F32),
            jax.ShapeDtypeStruct((depth, m, nh, ATT_V_DIM), F32),
        ],
        scratch_shapes=[pltpu.VMEM((tm, d), BF16), pltpu.VMEM((3, tm, LANES), F32),
                        pltpu.VMEM((2, tm, LANES), F32),
                        pltpu.VMEM((bw // LANES * tm, LANES), F32), pltpu.VMEM((bw // LANES * tm, LANES), F32)],
        input_output_aliases=aliases,
        compiler_params=_cparams(("arbitrary", "arbitrary")),
    )(*args)


def _mixout_kernel(x_ref, a_ref, h_ref, wa_ref, wh_ref, o_ref):
    o_ref[...] = (x_ref[...] + _dot(a_ref[...].astype(BF16), wa_ref[0])
                  + _dot(h_ref[...].astype(BF16), wh_ref[0]))


def _mixout(x, a, h, w_all, *, layer, tm):
    m, d = x.shape
    bw = a.shape[1]
    return pl.pallas_call(
        _mixout_kernel,
        grid=(m // tm,),
        in_specs=[
            pl.BlockSpec((tm, d), lambda i: (i, 0)),
            pl.BlockSpec((tm, bw), lambda i: (i, 0)),
            pl.BlockSpec((tm, bw), lambda i: (i, 0)),
            pl.BlockSpec((1, bw, d), lambda i: (layer, 0, 0)),
            pl.BlockSpec((1, bw, d), lambda i: (layer, 1, 0)),
        ],
        out_specs=pl.BlockSpec((tm, d), lambda i: (i, 0)),
        out_shape=jax.ShapeDtypeStruct((m, d), F32),
        compiler_params=_cparams(("parallel",)),
    )(x, a, h, w_all, w_all)


def _lam_from(lam_ref, lam_init):
    lv = lam_ref[...]
    s1 = jnp.sum(lv[0:1] * lv[1:2], axis=-1, keepdims=True)
    s2 = jnp.sum(lv[2:3] * lv[3:4], axis=-1, keepdims=True)
    return jnp.exp(s1) - jnp.exp(s2) + lam_init


def _attn_prompt_kernel(lam_ref, subln_ref, q_ref, k_ref, v_ref, o_ref, m_scr, acc_scr, *, tq, hp, lam_init):
    qi = pl.program_id(2)
    tk = tq
    m_scr[...] = jnp.full(m_scr.shape, NEG_INF, F32)
    acc_scr[...] = jnp.zeros(acc_scr.shape, F32)
    ones = jnp.ones((tk, LANES), BF16)
    qs = []
    for hh in range(hp):
        q2 = q_ref[:, hh * 2 * LANES:(hh + 1) * 2 * LANES]
        qs.append(jnp.concatenate([q2[:, :LANES], q2[:, LANES:]], axis=0))

    def chunk(j, masked):
        start = pl.multiple_of(j * tk, tk)
        for hh in range(hp):
            kc = k_ref[pl.ds(start, tk), hh * LANES:(hh + 1) * LANES]
            vc = v_ref[pl.ds(start, tk), hh * LANES:(hh + 1) * LANES]
            s = _dot_nt(qs[hh], kc)
            if masked:
                row = lax.broadcasted_iota(jnp.int32, s.shape, 0)
                row = jnp.where(row >= tq, row - tq, row)
                col = lax.broadcasted_iota(jnp.int32, s.shape, 1)
                s = jnp.where(col <= row, s, NEG_INF)
            m_prev = m_scr[hh]
            m_new = jnp.maximum(m_prev, jnp.max(s, axis=1, keepdims=True))
            alpha = jnp.exp(m_prev - m_new)
            p = jnp.exp(s - _lane_tile(m_new, tk // LANES))
            vx = jnp.concatenate([vc, ones], axis=1)
            acc_scr[hh] = _lane_tile(alpha, 2) * acc_scr[hh] + _dot(p.astype(BF16), vx)
            m_scr[hh] = m_new

    def body(j, carry):
        chunk(j, False)
        return carry

    lax.fori_loop(0, qi, body, 0)
    chunk(qi, True)

    lam = _lam_from(lam_ref, lam_init)
    for hh in range(hp):
        acc = acc_scr[hh]
        o1 = acc[:tq, :LANES] / acc[:tq, LANES:]
        o2 = acc[tq:, :LANES] / acc[tq:, LANES:]
        o = o1 - lam * o2
        o_ref[:, hh * LANES:(hh + 1) * LANES] = (_rms(o) * subln_ref[...] * (1.0 - lam_init)).astype(o_ref.dtype)


def _attn_prompt(lam, subln, qx, kvb, *, batch, seq, tq, hp, lam_init):
    m = qx.shape[0]
    bw = kvb.shape[1] // 2
    nh = bw // ATT_V_DIM
    ng = nh // hp
    nq = seq // tq
    kern = functools.partial(_attn_prompt_kernel, tq=tq, hp=hp, lam_init=lam_init)
    return pl.pallas_call(
        kern,
        grid=(batch, ng, nq),
        in_specs=[
            pl.BlockSpec(lam.shape, lambda b, h, i: (0, 0)),
            pl.BlockSpec((1, ATT_V_DIM), lambda b, h, i: (0, 0)),
            pl.BlockSpec((tq, hp * 2 * LANES), lambda b, h, i: (b * nq + i, h)),
            pl.BlockSpec((seq, hp * LANES), lambda b, h, i: (b, h)),
            pl.BlockSpec((seq, hp * LANES), lambda b, h, i: (b, ng + h)),
        ],
        out_specs=pl.BlockSpec((tq, hp * ATT_V_DIM), lambda b, h, i: (b * nq + i, h)),
        out_shape=jax.ShapeDtypeStruct((m, bw), BF16),
        scratch_shapes=[pltpu.VMEM((hp, 2 * tq, LANES), F32), pltpu.VMEM((hp, 2 * tq, 2 * LANES), F32)],
        compiler_params=_cparams(("parallel", "parallel", "arbitrary")),
    )(lam, subln, qx, kvb, kvb)


def _attn_sample_kernel(pt_ref, lam_ref, subln_ref, qm_ref, kn_ref, vn_ref, *rest,
                        n_pages, page, nh, lam_init):
    k_refs = rest[:n_pages]
    v_refs = rest[n_pages:2 * n_pages]
    o_ref = rest[2 * n_pages]
    s_scr = rest[2 * n_pages + 1]
    rows = page * nh
    qm = qm_ref[0]

    def head_mask(width):
        lane = lax.broadcasted_iota(jnp.int32, (nh, width), 1)
        sub = lax.broadcasted_iota(jnp.int32, (nh, width), 0)
        return (lane % nh) == sub

    def scores(kflat, width):
        st = _dot_nt(qm, kflat)
        st = st.reshape(2, nh, width)
        return jnp.sum(jnp.where(head_mask(width)[None], st, 0.0), axis=1)

    for j in range(n_pages):
        kf = k_refs[j][0, 0].reshape(rows, LANES).astype(BF16)
        s_scr[0:2, j * rows:(j + 1) * rows] = scores(kf, rows)
    pad = jnp.zeros((LANES - nh, LANES), F32)
    knp = jnp.concatenate([kn_ref[0, 0], pad], axis=0).astype(BF16)
    s_new = scores(knp, LANES)
    lane = lax.broadcasted_iota(jnp.int32, s_new.shape, 1)
    s_scr[0:2, n_pages * rows:] = jnp.where(lane < nh, s_new, NEG_INF)

    s_all = s_scr[0:2, :]
    ncol = s_all.shape[1] // LANES

    def per_head_allreduce(x, op):
        r = x[:, :LANES]
        for c in range(1, ncol):
            r = op(r, x[:, c * LANES:(c + 1) * LANES])
        sh = nh
        while sh < LANES:
            r = op(r, pltpu.roll(r, sh, 1))
            sh *= 2
        return r

    mx = per_head_allreduce(s_all, jnp.maximum)
    p = jnp.exp(s_all - _lane_tile(mx, ncol))
    den = per_head_allreduce(p, jnp.add)
    pn = p / _lane_tile(den, ncol)
    w = pn[0:1] - _lam_from(lam_ref, lam_init) * pn[1:2]

    def weighted(wj, vflat, width):
        a = jnp.where(head_mask(width), jnp.broadcast_to(wj, (nh, width)), 0.0)
        return _dot(a.astype(BF16), vflat)

    o = jnp.zeros((nh, LANES), F32)
    for j in range(n_pages):
        vf = v_refs[j][0, 0].reshape(rows, LANES).astype(BF16)
        o = o + weighted(w[:, j * rows:(j + 1) * rows], vf, rows)
    vnp = jnp.concatenate([vn_ref[0, 0], pad], axis=0).astype(BF16)
    o = o + weighted(w[:, n_pages * rows:], vnp, LANES)
    o_ref[0] = _rms(o) * subln_ref[...] * (1.0 - lam_init)


def _attn_sample(page_table, lam, subln, qm, kf_all, vf_all, cache_k, cache_v, *, layer, lam_init):
    db, n_pages = page_table.shape
    _, _, page, nh, vd = cache_k.shape
    rows = page * nh

    def page_spec(j):
        return pl.BlockSpec((1, 1, page, nh, vd), lambda b, pt: (layer, pt[b, j], 0, 0, 0))

    kern = functools.partial(_attn_sample_kernel, n_pages=n_pages, page=page, nh=nh, lam_init=lam_init)
    grid_spec = pltpu.PrefetchScalarGridSpec(
        num_scalar_prefetch=1,
        grid=(db,),
        in_specs=[
            pl.BlockSpec(lam.shape, lambda b, pt: (0, 0)),
            pl.BlockSpec((1, vd), lambda b, pt: (0, 0)),
            pl.BlockSpec((1, 2 * nh, LANES), lambda b, pt: (b, 0, 0)),
            pl.BlockSpec((1, 1, nh, vd), lambda b, pt: (layer, b, 0, 0)),
            pl.BlockSpec((1, 1, nh, vd), lambda b, pt: (layer, b, 0, 0)),
        ] + [page_spec(j) for j in range(n_pages)] + [page_spec(j) for j in range(n_pages)],
        out_specs=pl.BlockSpec((1, nh, vd), lambda b, pt: (b, 0, 0)),
        scratch_shapes=[pltpu.VMEM((SUBLANES, n_pages * rows + LANES), F32)],
    )
    return pl.pallas_call(
        kern,
        grid_spec=grid_spec,
        out_shape=jax.ShapeDtypeStruct((db, nh, vd), F32),
        compiler_params=_cparams(("arbitrary",)),
    )(page_table, lam, subln, qm, kf_all, vf_all, *([cache_k] * n_pages), *([cache_v] * n_pages))


def _mlstm_prompt_kernel(q_ref, k_ref, v_ref, mo_ref, gt_ref, gain_ref,
                         h_ref, c_out_ref, n_out_ref, m_out_ref,
                         c_scr, n_scr, m_scr, *, c, dh):
    nmh = N_M_HEADS
    kk = pl.program_id(1)

    @pl.when(kk == 0)
    def _():
        c_scr[...] = jnp.zeros(c_scr.shape, F32)
        n_scr[...] = jnp.zeros(n_scr.shape, F32)
        m_scr[...] = jnp.zeros(m_scr.shape, F32)

    rows = gt_ref[...]
    r_hi, r_lo = _split_hi_lo(rows)
    ti = lax.broadcasted_iota(jnp.int32, (c, c), 0)
    si = lax.broadcasted_iota(jnp.int32, (c, c), 1)
    tril = (si <= ti)
    lower = tril.astype(BF16)
    upper = (ti <= si).astype(BF16)
    eye = (ti == si).astype(BF16)
    cum_rows = _dot(r_hi, upper) + _dot(r_lo, upper)
    cum_cols = _dot_nt(lower, r_hi) + _dot_nt(lower, r_lo)
    id_cols = _dot_nt(eye, r_hi) + _dot_nt(eye, r_lo)

    for h in range(nmh):
        hs = slice(h * dh, (h + 1) * dh)
        b_row = cum_rows[nmh + h:nmh + h + 1, :]
        ig_row = rows[h:h + 1, :]
        b_col = cum_cols[:, nmh + h:nmh + h + 1]
        ig_col = id_cols[:, h:h + 1]
        b_last = b_col[c - 1:c, :]
        m_prev = m_scr[h, 0:1, 0:1]

        d_mat = jnp.where(tril, b_col - b_row + ig_row, NEG_INF)
        inter = b_col + m_prev
        m_t = jnp.maximum(inter, jnp.max(d_mat, axis=1, keepdims=True))
        w_intra = jnp.exp(d_mat - m_t)
        w_inter = jnp.exp(inter - m_t)

        qb = q_ref[:, hs].astype(BF16)
        kf = k_ref[:, hs] * (dh ** -0.5)
        kb = kf.astype(BF16)
        vb = v_ref[:, hs].astype(BF16)
        a = _dot_nt(qb, kb) * w_intra
        c_prev = c_scr[h]
        n_prev = n_scr[h]
        num = _dot(a.astype(BF16), vb) + w_inter * _dot(qb, c_prev.astype(BF16))
        qn = jnp.sum(qb.astype(F32) * n_prev.astype(BF16).astype(F32), axis=1, keepdims=True)
        den = jnp.sum(a, axis=1, keepdims=True) + w_inter * qn
        den = jnp.maximum(jnp.abs(den), jnp.exp(-m_t))
        hh = num / den

        m_new = m_t[c - 1:c, :]
        w_s = jnp.exp(b_last - b_col + ig_col - m_new)
        decay = jnp.exp(b_last + m_prev - m_new)
        kw = kf * w_s
        c_new = decay * c_prev + _dot_tn(kw.astype(BF16), vb)
        n_new = decay * n_prev + jnp.sum(kw, axis=0, keepdims=True)
        c_scr[h] = c_new
        n_scr[h] = n_new
        m_scr[h] = jnp.broadcast_to(m_new, (SUBLANES, LANES))
        c_out_ref[0, h] = c_new
        n_out_ref[0, h] = n_new
        m_out_ref[0, h] = jnp.broadcast_to(m_new, (1, LANES))

        h_ref[:, hs] = (_rms(hh) * gain_ref[:, hs] * _sigmoid(mo_ref[:, hs])).astype(h_ref.dtype)


def _mlstm_prompt(proj, gt, gain, *, batch, seq, c):
    m = proj.shape[0]
    bw = proj.shape[1] // 4
    nmh = N_M_HEADS
    dh = bw // nmh
    nc = seq // c
    kern = functools.partial(_mlstm_prompt_kernel, c=c, dh=dh)

    def col(base):
        return lambda b, k: (b * nc + k, base)

    return pl.pallas_call(
        kern,
        grid=(batch, nc),
        in_specs=[
            pl.BlockSpec((c, bw), col(0)),
            pl.BlockSpec((c, bw), col(1)),
            pl.BlockSpec((c, bw), col(2)),
            pl.BlockSpec((c, bw), col(3)),
            pl.BlockSpec((N_GATES, c), lambda b, k: (0, b * nc + k)),
            pl.BlockSpec((1, bw), lambda b, k: (0, 0)),
        ],
        out_specs=[
            pl.BlockSpec((c, bw), lambda b, k: (b * nc + k, 0)),
            pl.BlockSpec((1, nmh, dh, dh), lambda b, k: (b, 0, 0, 0)),
            pl.BlockSpec((1, nmh, 1, dh), lambda b, k: (b, 0, 0, 0)),
            pl.BlockSpec((1, nmh, 1, LANES), lambda b, k: (b, 0, 0, 0)),
        ],
        out_shape=[
            jax.ShapeDtypeStruct((m, bw), BF16),
            jax.ShapeDtypeStruct((batch, nmh, dh, dh), F32),
            jax.ShapeDtypeStruct((batch, nmh, 1, dh), F32),
            jax.ShapeDtypeStruct((batch, nmh, 1, LANES), F32),
        ],
        scratch_shapes=[pltpu.VMEM((nmh, dh, dh), F32), pltpu.VMEM((nmh, 1, dh), F32),
                        pltpu.VMEM((nmh, SUBLANES, LANES), F32)],
        compiler_params=_cparams(("parallel", "arbitrary")),
    )(proj, proj, proj, proj, gt, gain)


def _mlstm_sample_kernel(*refs, bs, bw, dh, aliased):
    p_ref, g_ref, c_ref, n_ref, m_ref, gain_ref = refs[:6]
    h_ref, c_out_ref, n_out_ref, m_out_ref = refs[6 + (1 if aliased else 0):]
    nmh = N_M_HEADS
    g = g_ref[...]
    m_all = m_ref[0]
    rowid = lax.broadcasted_iota(jnp.int32, (bs, dh), 0)
    for h in range(nmh):
        def cols(base):
            return p_ref[:, base * bw + h * dh: base * bw + (h + 1) * dh]
        q, k, v, mo = cols(0), cols(1) * (dh ** -0.5), cols(2), cols(3)
        ig = g[:, h:h + 1]
        lf = g[:, nmh + h:nmh + h + 1]
        m_prev = m_all[:, h:h + 1]
        m_t = jnp.maximum(lf + m_prev, ig)
        w_i = jnp.exp(ig - m_t)
        w_f = jnp.exp(lf + m_prev - m_t)
        qb = q.astype(BF16)
        qf = qb.astype(F32)
        kb = k.astype(BF16).astype(F32)
        vb = v.astype(BF16)
        n_prev = n_ref[0][:, h * dh:(h + 1) * dh]
        a = jnp.sum(qf * kb, axis=1, keepdims=True) * w_i
        qn = jnp.sum(qf * n_prev.astype(BF16).astype(F32), axis=1, keepdims=True)
        kw = k * w_i
        qc = jnp.zeros((bs, dh), F32)
        for j in range(bs):
            c_prev = c_ref[0, j, h]
            qc = jnp.where(rowid == j, _dot(qb, c_prev.astype(BF16)), qc)
            kwj = jnp.where(rowid == j, kw, 0.0).astype(BF16)
            c_out_ref[0, j, h] = w_f[j:j + 1, :] * c_prev + _dot_tn(kwj, vb)
        num = a * vb.astype(F32) + w_f * qc
        den = a + w_f * qn
        den = jnp.maximum(jnp.abs(den), jnp.exp(-m_t))
        hh = num / den
        n_out_ref[:, h * dh:(h + 1) * dh] = w_f * n_prev + kw
        m_out_ref[:, h:h + 1] = m_t
        gain = gain_ref[:, h * dh:(h + 1) * dh]
        h_ref[:, h * dh:(h + 1) * dh] = _rms(hh) * gain * _sigmoid(mo)


def _mlstm_sample(proj, g, state_c, state_n2, state_m, gain, c_prev_out, *, layer, bs):
    db = proj.shape[0]
    depth = state_c.shape[0]
    bw = proj.shape[1] // 4
    nmh = N_M_HEADS
    dh = bw // nmh
    aliased = c_prev_out is not None
    kern = functools.partial(_mlstm_sample_kernel, bs=bs, bw=bw, dh=dh, aliased=aliased)
    in_specs = [
        pl.BlockSpec((bs, 4 * bw), lambda i: (i, 0)),
        pl.BlockSpec((bs, N_GATES), lambda i: (i, 0)),
        pl.BlockSpec((1, bs, nmh, dh, dh), lambda i: (layer, i, 0, 0, 0)),
        pl.BlockSpec((1, bs, bw), lambda i: (layer, i, 0)),
        pl.BlockSpec((1, bs, nmh), lambda i: (layer, i, 0)),
        pl.BlockSpec((1, bw), lambda i: (0, 0)),
    ]
    args = [proj, g, state_c, state_n2, state_m, gain]
    aliases = {}
    if aliased:
        in_specs.append(_any_spec())
        args.append(c_prev_out)
        aliases = {6: 1}
    return pl.pallas_call(
        kern,
        grid=(db // bs,),
        in_specs=in_specs,
        out_specs=[
            pl.BlockSpec((bs, bw), lambda i: (i, 0)),
            pl.BlockSpec((1, bs, nmh, dh, dh), lambda i: (layer, i, 0, 0, 0)),
            pl.BlockSpec((bs, bw), lambda i: (i, 0)),
            pl.BlockSpec((bs, nmh), lambda i: (i, 0)),
        ],
        out_shape=[
            jax.ShapeDtypeStruct((db, bw), F32),
            jax.ShapeDtypeStruct((depth, db, nmh, dh, dh), F32),
            jax.ShapeDtypeStruct((db, bw), F32),
            jax.ShapeDtypeStruct((db, nmh), F32),
        ],
        input_output_aliases=aliases,
        compiler_params=_cparams(("parallel",)),
    )(*args)


def _pick_tile(n, prefs):
    for t in prefs:
        if n % t == 0:
            return t
    return n


def kernel(x_prompt, x_sample, cache_k, cache_v, state_C, state_n, state_m, page_table, norm_ffn1, ffn1_w_in, ffn1_w_out, norm_mix, w_mix_in, b_gates, lam_q1, lam_k1, lam_q2, lam_k2, attn_subln, mlstm_gain, w_mix_out, norm_ffn2, ffn2_w_in, ffn2_w_out, norm_final):
    batch, seq, d = x_prompt.shape
    db, t_dec, _ = x_sample.shape
    depth = cache_k.shape[0]
    page = cache_k.shape[2]
    nh = cache_k.shape[3]
    past = page_table.shape[1] * page
    bw = d // 2
    nmh = N_M_HEADS
    dh = bw // nmh
    nmain = N_MIX_BLOCKS * bw
    assert t_dec == 1 and nh * ATT_V_DIM == bw and w_mix_in.shape[2] == nmain + N_GATES

    mp = batch * seq
    tm_p = _pick_tile(mp, (512, 256, 128))
    tm_s = db
    tm_f = _pick_tile(mp, (1024, 512, 256, 128))
    tf = 256
    tq = _pick_tile(seq, (512, 256, 128))
    hp = 4 if nh % 4 == 0 else 1
    chunk = _pick_tile(seq, (256, 128))
    bs = SUBLANES

    xp = x_prompt.reshape(mp, d)
    xs = x_sample.reshape(db, d)
    tab = _rope_table()
    gf = norm_final.reshape(1, d)
    state_n2 = state_n.reshape(depth, db, bw)

    w1_in, w1_out = ffn1_w_in, ffn1_w_out
    w2_in, w2_out = ffn2_w_in, ffn2_w_out
    wmix = w_mix_in.astype(BF16)
    wout = w_mix_out.astype(BF16)
    wg_small = lax.optimization_barrier(w_mix_in[:, :, nmain:])

    kf_p = vf_p = kf_s = vf_s = c_s_all = None
    outs_p = [[] for _ in range(3)]
    outs_s = [[] for _ in range(2)]
    for l in range(depth):
        lam_init = 0.8 - 0.6 * math.exp(-0.3 * l)
        last = l == depth - 1
        wgt_hi, wgt_lo = _split_hi_lo(wg_small[l].T)
        wgt = jnp.concatenate([wgt_hi, wgt_lo], axis=0)
        bg = jnp.broadcast_to(b_gates[l].reshape(N_GATES, 1), (N_GATES, LANES))
        lam = jnp.stack([lam_q1[l], lam_k1[l], lam_q2[l], lam_k2[l]])
        subln = attn_subln[l].reshape(1, ATT_V_DIM)
        gain = mlstm_gain[l].reshape(1, bw)
        g1 = norm_ffn1[l].reshape(1, d)
        gm = norm_mix[l].reshape(1, d)
        g2 = norm_ffn2[l].reshape(1, d)

        xp, xs = _ffn(xp, xs, g1, w1_in, w1_out, gf, layer=l, tm=tm_f, tf=tf, final_norm=False)

        proj_p, qx_p, kvb_p, gt_p, kf_p, vf_p = _mixin(
            xp, gm, wmix, wgt, bg, tab, kf_p, vf_p, layer=l, depth=depth, tm=tm_p, seq=seq, pos_offset=0, pos_step=1)
        proj_s, qx_s, kvb_s, gt_s, kf_s, vf_s = _mixin(
            xs, gm, wmix, wgt, bg, tab, kf_s, vf_s, layer=l, depth=depth, tm=tm_s, seq=seq, pos_offset=past, pos_step=0)

        a_p = _attn_prompt(lam, subln, qx_p, kvb_p, batch=batch, seq=seq, tq=tq, hp=hp, lam_init=lam_init)
        h_p, c_p, n_p, m_p = _mlstm_prompt(proj_p, gt_p, gain, batch=batch, seq=seq, c=chunk)

        qm = qx_s.reshape(db, nh, 2, LANES).transpose(0, 2, 1, 3).reshape(db, 2 * nh, LANES)
        a_s = _attn_sample(page_table, lam, subln, qm, kf_s, vf_s, cache_k, cache_v, layer=l, lam_init=lam_init)
        h_s, c_s_all, n_s, m_s = _mlstm_sample(proj_s, gt_s.T, state_C, state_n2, state_m, gain, c_s_all, layer=l, bs=bs)

        xp = _mixout(xp, a_p, h_p, wout, layer=l, tm=tm_p)
        xs = _mixout(xs, a_s.reshape(db, bw), h_s, wout, layer=l, tm=tm_s)

        xp, xs = _ffn(xp, xs, g2, w2_in, w2_out, gf, layer=l, tm=tm_f, tf=tf, final_norm=last)

        outs_p[0].append(c_p)
        outs_p[1].append(n_p.reshape(batch, nmh, dh))
        outs_p[2].append(m_p[:, :, 0, 0])
        outs_s[0].append(n_s.reshape(db, nmh, dh))
        outs_s[1].append(m_s)

    y_prompt = xp.reshape(batch, seq, d)
    y_sample = xs.reshape(db, 1, d)
    return (y_prompt, y_sample,
            kf_p.reshape(depth, batch, seq, nh, ATT_V_DIM), vf_p.reshape(depth, batch, seq, nh, ATT_V_DIM),
            *[jnp.stack(o) for o in outs_p],
            kf_s.reshape(depth, db, 1, nh, ATT_V_DIM), vf_s.reshape(depth, db, 1, nh, ATT_V_DIM),
            c_s_all, *[jnp.stack(o) for o in outs_s])
```

```python
import functools
import math

import jax
import jax.numpy as jnp
import numpy as np
from jax import lax
from jax.experimental import pallas as pl
from jax.experimental.pallas import tpu as pltpu

F32 = jnp.float32
BF16 = jnp.bfloat16

EPS = 1e-6
ATT_HEAD_DIM = 64
ATT_V_DIM = 2 * ATT_HEAD_DIM
ATT_SCALE = ATT_HEAD_DIM ** -0.5
ROT_DIM = ATT_HEAD_DIM // 4
ROPE_THETA = 500000.0
N_M_HEADS = 4
N_GATES = 2 * N_M_HEADS
N_MIX_BLOCKS = 7

LANES = 128
SUBLANES = 8
VMEM_LIMIT_BYTES = 56 * 1024 * 1024

NEG_INF = float("-inf")


def _cparams(semantics):
    return pltpu.CompilerParams(dimension_semantics=semantics, vmem_limit_bytes=VMEM_LIMIT_BYTES)


def _rms(x):
    return x * lax.rsqrt(jnp.mean(x * x, axis=-1, keepdims=True) + EPS)


def _sigmoid(x):
    return 1.0 / (1.0 + jnp.exp(-x))


def _dot(a, b):
    return jnp.dot(a, b, preferred_element_type=F32)


def _dot_nt(a, b):
    return lax.dot_general(a, b, (((1,), (1,)), ((), ())), preferred_element_type=F32)


def _dot_tn(a, b):
    return lax.dot_general(a, b, (((0,), (0,)), ((), ())), preferred_element_type=F32)


def _split_hi_lo(x):
    hi = x.astype(BF16)
    lo = (x - hi.astype(F32)).astype(BF16)
    return hi, lo


def _lane_tile(x, n):
    return jnp.concatenate([x] * n, axis=1)


def _any_spec():
    return pl.BlockSpec(memory_space=pl.ANY)


def _ffn_kernel(*refs, tf, rem, final_norm):
    nsub = tf // LANES
    x_ref, xs_ref, g_ref, wg_ref = refs[:4]
    wu_refs = refs[4:4 + nsub]
    wo_ref, gf_ref, o_ref, os_ref, xn_ref, xsn_ref = refs[4 + nsub:]
    i = pl.program_id(0)
    j = pl.program_id(1)
    nj = pl.num_programs(1)

    def init(src_ref, n_ref, dst_ref):
        x = src_ref[...]
        n_ref[...] = (_rms(x) * g_ref[...]).astype(BF16)
        dst_ref[...] = x

    @pl.when(j == 0)
    def _():
        init(x_ref, xn_ref, o_ref)

    @pl.when((j == 0) & (i == 0))
    def _():
        init(xs_ref, xsn_ref, os_ref)

    def step(width):
        wg = wg_ref[0, :, :width].astype(BF16)
        wu = jnp.concatenate([r[0].astype(BF16) for r in wu_refs[:width // LANES]], axis=1)
        wo = wo_ref[0, :width, :].astype(BF16)

        def rows(n_ref, dst_ref):
            xn = n_ref[...]
            gate = _dot(xn, wg)
            up = _dot(xn, wu)
            act = (gate * _sigmoid(gate)) * up * 0.5
            dst_ref[...] += _dot(act.astype(BF16), wo)

        rows(xn_ref, o_ref)

        @pl.when(i == 0)
        def _():
            rows(xsn_ref, os_ref)

    if rem == tf:
        step(tf)
    else:
        @pl.when(j < nj - 1)
        def _():
            step(tf)

        @pl.when(j == nj - 1)
        def _():
            step(rem)

    if final_norm:
        @pl.when(j == nj - 1)
        def _():
            o_ref[...] = _rms(o_ref[...]) * gf_ref[...]

        @pl.when((j == nj - 1) & (i == 0))
        def _():
            os_ref[...] = _rms(os_ref[...]) * gf_ref[...]


def _ffn(x, xs, g, w_in, w_out, gf, *, layer, tm, tf, final_norm):
    m, d = x.shape
    ms = xs.shape[0]
    f = w_out.shape[1]
    assert f % LANES == 0 and tf % LANES == 0
    nj = (f + tf - 1) // tf
    rem = f - (nj - 1) * tf
    nsub = tf // LANES
    last_blk = 2 * f // LANES - 1

    def up_spec(r):
        return pl.BlockSpec((1, d, LANES),
                            lambda i, j: (layer, 0, jnp.minimum(f // LANES + nsub * j + r, last_blk)))

    return pl.pallas_call(
        functools.partial(_ffn_kernel, tf=tf, rem=rem, final_norm=final_norm),
        grid=(m // tm, nj),
        in_specs=[
            pl.BlockSpec((tm, d), lambda i, j: (i, 0)),
            pl.BlockSpec((ms, d), lambda i, j: (0, 0)),
            pl.BlockSpec((1, d), lambda i, j: (0, 0)),
            pl.BlockSpec((1, d, tf), lambda i, j: (layer, 0, j)),
        ] + [up_spec(r) for r in range(nsub)] + [
            pl.BlockSpec((1, tf, d), lambda i, j: (layer, j, 0)),
            pl.BlockSpec((1, d), lambda i, j: (0, 0)),
        ],
        out_specs=[
            pl.BlockSpec((tm, d), lambda i, j: (i, 0)),
            pl.BlockSpec((ms, d), lambda i, j: (0, 0)),
        ],
        out_shape=[jax.ShapeDtypeStruct((m, d), F32), jax.ShapeDtypeStruct((ms, d), F32)],
        scratch_shapes=[pltpu.VMEM((tm, d), BF16), pltpu.VMEM((ms, d), BF16)],
        compiler_params=_cparams(("arbitrary", "arbitrary")),
    )(x, xs, g, w_in, *([w_in] * nsub), w_out, gf)


def _rope_table():
    half = ROT_DIM // 2
    inv = ROPE_THETA ** (-jnp.arange(0, ROT_DIM, 2, dtype=F32) / ROT_DIM)
    r = np.arange(LANES) % ATT_HEAD_DIM
    inv_l = jnp.where(r < ROT_DIM, inv[r % half], 0.0)
    tab = jnp.zeros((SUBLANES, LANES), F32)
    tab = tab.at[0].set(inv_l)
    tab = tab.at[1].set(jnp.asarray(np.where(r < half, -1.0, 0.0), F32))
    tab = tab.at[2].set(jnp.asarray(np.where((r >= half) & (r < ROT_DIM), 1.0, 0.0), F32))
    return tab


def _mixin_kernel(*refs, tm, bw, seq, pos_offset, pos_step, aliased):
    x_ref, g_ref, w_ref, wgt_ref, bg_ref, tab_ref = refs[:6]
    outs = refs[6 + (2 if aliased else 0):]
    proj_ref, qx_ref, kvb_ref, gt_ref, kf_ref, vf_ref, xn_ref, cs_ref, base_ref, raw_a, raw_b = outs
    assert N_MIX_BLOCKS == 7
    i = pl.program_id(0)
    j = pl.program_id(1)
    half = ROT_DIM // 2
    ncol = bw // LANES

    def matmul():
        return _dot(xn_ref[...], w_ref[0])

    def put_raw(raw_ref, y):
        for c in range(ncol):
            raw_ref[c * tm:(c + 1) * tm, :] = y[:, c * LANES:(c + 1) * LANES]

    def rotate(raw_ref):
        cols = []
        for c in range(ncol):
            yc = raw_ref[c * tm:(c + 1) * tm, :]
            cols.append(yc * cs_ref[0]
                        + pltpu.roll(yc, LANES - half, 1) * cs_ref[1]
                        + pltpu.roll(yc, half, 1) * cs_ref[2])
        return cols

    @pl.when((i == 0) & (j == 0))
    def _():
        off = (pos_step * lax.broadcasted_iota(jnp.int32, (tm, LANES), 0)).astype(F32)
        ang_b = off * tab_ref[0:1, :]
        base_ref[0] = jnp.cos(ang_b)
        base_ref[1] = jnp.sin(ang_b)

    @pl.when(j == 0)
    def _():
        xn = _rms(x_ref[...]) * g_ref[...]
        x_hi, x_lo = _split_hi_lo(xn)
        xn_ref[...] = x_hi
        put_raw(raw_a, _dot(x_hi, w_ref[0]))
        row0 = ((i * tm) % seq) * pos_step + pos_offset
        ang_a = row0.astype(F32) * tab_ref[0:1, :]
        cos_a = jnp.cos(ang_a)
        sin_a = jnp.sin(ang_a)
        cos = cos_a * base_ref[0] - sin_a * base_ref[1]
        sin = sin_a * base_ref[0] + cos_a * base_ref[1]
        cs_ref[0] = cos
        cs_ref[1] = sin * tab_ref[1:2, :]
        cs_ref[2] = sin * tab_ref[2:3, :]
        wst = wgt_ref[...]
        r1 = _dot_nt(wst, x_hi)
        r2 = _dot_nt(wst, x_lo)
        pre = r1[:N_GATES] + r1[N_GATES:] + r2[:N_GATES] + bg_ref[...][:, :1]
        logsig = jnp.minimum(pre, 0.0) - jnp.log1p(jnp.exp(-jnp.abs(pre)))
        is_f = lax.broadcasted_iota(jnp.int32, pre.shape, 0) >= N_M_HEADS
        gt_ref[...] = jnp.where(is_f, logsig, pre)

    @pl.when(j == 1)
    def _():
        put_raw(raw_b, matmul())
        cols = rotate(raw_a)
        lane = lax.broadcasted_iota(jnp.int32, (tm, LANES), 1)
        first = lane < ATT_HEAD_DIM
        for c, yc in enumerate(cols):
            yc = yc * ATT_SCALE
            qx_ref[:, (2 * c) * LANES:(2 * c + 1) * LANES] = jnp.where(first, yc, 0.0).astype(BF16)
            qx_ref[:, (2 * c + 1) * LANES:(2 * c + 2) * LANES] = jnp.where(first, 0.0, yc).astype(BF16)

    @pl.when(j == 2)
    def _():
        put_raw(raw_a, matmul())
        cols = rotate(raw_b)
        for c, yc in enumerate(cols):
            raw_b[c * tm:(c + 1) * tm, :] = yc
            kvb_ref[:, c * LANES:(c + 1) * LANES] = yc.astype(BF16)

    def gather_tokens(dst_ref, raw_ref, part):
        lo, hi = (0, tm // 2) if part == 0 else (tm // 2, tm)
        blk = jnp.stack([raw_ref[c * tm + lo:c * tm + hi, :] for c in range(ncol)], axis=0)
        dst_ref[0, lo:hi] = jnp.swapaxes(blk, 0, 1)

    @pl.when(j == 3)
    def _():
        proj_ref[...] = matmul()
        for c in range(ncol):
            kvb_ref[:, c * LANES:(c + 1) * LANES] = raw_a[c * tm:(c + 1) * tm, :].astype(BF16)
        gather_tokens(kf_ref, raw_b, 0)

    @pl.when(j == 4)
    def _():
        proj_ref[...] = matmul()
        gather_tokens(kf_ref, raw_b, 1)

    @pl.when(j == 5)
    def _():
        proj_ref[...] = matmul()
        gather_tokens(vf_ref, raw_a, 0)

    @pl.when(j == 6)
    def _():
        proj_ref[...] = matmul()
        gather_tokens(vf_ref, raw_a, 1)


def _mixin(x, g, w_all, wgt, bg, tab, kf_prev, vf_prev, *, layer, depth, tm, seq, pos_offset, pos_step):
    m, d = x.shape
    bw = d // 2
    nh = bw // ATT_V_DIM
    aliased = kf_prev is not None
    kern = functools.partial(_mixin_kernel, tm=tm, bw=bw, seq=seq, pos_offset=pos_offset,
                             pos_step=pos_step, aliased=aliased)
    in_specs = [
        pl.BlockSpec((tm, d), lambda i, j: (i, 0)),
        pl.BlockSpec((1, d), lambda i, j: (0, 0)),
        pl.BlockSpec((1, d, bw), lambda i, j: (layer, 0, j)),
        pl.BlockSpec((2 * N_GATES, d), lambda i, j: (0, 0)),
        pl.BlockSpec((N_GATES, LANES), lambda i, j: (0, 0)),
        pl.BlockSpec((SUBLANES, LANES), lambda i, j: (0, 0)),
    ]
    args = [x, g, w_all, wgt, bg, tab]
    aliases = {}
    if aliased:
        in_specs += [_any_spec(), _any_spec()]
        args += [kf_prev, vf_prev]
        aliases = {6: 4, 7: 5}
    return pl.pallas_call(
        kern,
        grid=(m // tm, N_MIX_BLOCKS),
        in_specs=in_specs,
        out_specs=[
            pl.BlockSpec((tm, bw), lambda i, j: (i, jnp.clip(j - 3, 0, 3))),
            pl.BlockSpec((tm, 2 * bw), lambda i, j: (i, 0)),
            pl.BlockSpec((tm, bw), lambda i, j: (i, jnp.clip(j - 2, 0, 1))),
            pl.BlockSpec((N_GATES, tm), lambda i, j: (0, i)),
            pl.BlockSpec((1, tm, nh, ATT_V_DIM), lambda i, j: (layer, i, 0, 0)),
            pl.BlockSpec((1, tm, nh, ATT_V_DIM), lambda i, j: (layer, i, 0, 0)),
        ],
        out_shape=[
            jax.ShapeDtypeStruct((m, 4 * bw), F32),
            jax.ShapeDtypeStruct((m, 2 * bw), BF16),
            jax.ShapeDtypeStruct((m, 2 * bw), BF16),
            jax.ShapeDtypeStruct((N_GATES, m), F32),
            jax.ShapeDtypeStruct((depth, m, nh, ATT_V_DIM), F32),
            jax.ShapeDtypeStruct((depth, m, nh, ATT_V_DIM), F32),
        ],
        scratch_shapes=[pltpu.VMEM((tm, d), BF16), pltpu.VMEM((3, tm, LANES), F32),
                        pltpu.VMEM((2, tm, LANES), F32),
                        pltpu.VMEM((bw // LANES * tm, LANES), F32), pltpu.VMEM((bw // LANES * tm, LANES), F32)],
        input_output_aliases=aliases,
        compiler_params=_cparams(("arbitrary", "arbitrary")),
    )(*args)


def _mixout_kernel(x_ref, a_ref, h_ref, wa_ref, wh_ref, o_ref):
    o_ref[...] = (x_ref[...] + _dot(a_ref[...].astype(BF16), wa_ref[0])
                  + _dot(h_ref[...].astype(BF16), wh_ref[0]))


def _mixout(x, a, h, w_all, *, layer, tm):
    m, d = x.shape
    bw = a.shape[1]
    return pl.pallas_call(
        _mixout_kernel,
        grid=(m // tm,),
        in_specs=[
            pl.BlockSpec((tm, d), lambda i: (i, 0)),
            pl.BlockSpec((tm, bw), lambda i: (i, 0)),
            pl.BlockSpec((tm, bw), lambda i: (i, 0)),
            pl.BlockSpec((1, bw, d), lambda i: (layer, 0, 0)),
            pl.BlockSpec((1, bw, d), lambda i: (layer, 1, 0)),
        ],
        out_specs=pl.BlockSpec((tm, d), lambda i: (i, 0)),
        out_shape=jax.ShapeDtypeStruct((m, d), F32),
        compiler_params=_cparams(("parallel",)),
    )(x, a, h, w_all, w_all)


def _lam_from(lam_ref, lam_init):
    lv = lam_ref[...]
    s1 = jnp.sum(lv[0:1] * lv[1:2], axis=-1, keepdims=True)
    s2 = jnp.sum(lv[2:3] * lv[3:4], axis=-1, keepdims=True)
    return jnp.exp(s1) - jnp.exp(s2) + lam_init


def _attn_prompt_kernel(lam_ref, subln_ref, q_ref, k_ref, v_ref, o_ref, m_scr, acc_scr, *, tq, hp, lam_init):
    qi = pl.program_id(2)
    tk = tq
    m_scr[...] = jnp.full(m_scr.shape, NEG_INF, F32)
    acc_scr[...] = jnp.zeros(acc_scr.shape, F32)
    ones = jnp.ones((tk, LANES), BF16)
    qs = []
    for hh in range(hp):
        q2 = q_ref[:, hh * 2 * LANES:(hh + 1) * 2 * LANES]
        qs.append(jnp.concatenate([q2[:, :LANES], q2[:, LANES:]], axis=0))

    def chunk(j, masked):
        start = pl.multiple_of(j * tk, tk)
        for hh in range(hp):
            kc = k_ref[pl.ds(start, tk), hh * LANES:(hh + 1) * LANES]
            vc = v_ref[pl.ds(start, tk), hh * LANES:(hh + 1) * LANES]
            s = _dot_nt(qs[hh], kc)
            if masked:
                row = lax.broadcasted_iota(jnp.int32, s.shape, 0)
                row = jnp.where(row >= tq, row - tq, row)
                col = lax.broadcasted_iota(jnp.int32, s.shape, 1)
                s = jnp.where(col <= row, s, NEG_INF)
            m_prev = m_scr[hh]
            m_new = jnp.maximum(m_prev, jnp.max(s, axis=1, keepdims=True))
            alpha = jnp.exp(m_prev - m_new)
            p = jnp.exp(s - _lane_tile(m_new, tk // LANES))
            vx = jnp.concatenate([vc, ones], axis=1)
            acc_scr[hh] = _lane_tile(alpha, 2) * acc_scr[hh] + _dot(p.astype(BF16), vx)
            m_scr[hh] = m_new

    def body(p, carry):
        chunk(2 * p, False)
        chunk(2 * p + 1, False)
        return carry

    lax.fori_loop(0, qi // 2, body, 0)

    @pl.when(qi % 2 == 1)
    def _():
        chunk(qi - 1, False)

    chunk(qi, True)

    lam = _lam_from(lam_ref, lam_init)
    for hh in range(hp):
        acc = acc_scr[hh]
        o1 = acc[:tq, :LANES] / acc[:tq, LANES:]
        o2 = acc[tq:, :LANES] / acc[tq:, LANES:]
        o = o1 - lam * o2
        o_ref[:, hh * LANES:(hh + 1) * LANES] = (_rms(o) * subln_ref[...] * (1.0 - lam_init)).astype(o_ref.dtype)


def _attn_prompt(lam, subln, qx, kvb, *, batch, seq, tq, hp, lam_init):
    m = qx.shape[0]
    bw = kvb.shape[1] // 2
    nh = bw // ATT_V_DIM
    ng = nh // hp
    nq = seq // tq
    kern = functools.partial(_attn_prompt_kernel, tq=tq, hp=hp, lam_init=lam_init)
    return pl.pallas_call(
        kern,
        grid=(batch, ng, nq),
        in_specs=[
            pl.BlockSpec(lam.shape, lambda b, h, i: (0, 0)),
            pl.BlockSpec((1, ATT_V_DIM), lambda b, h, i: (0, 0)),
            pl.BlockSpec((tq, hp * 2 * LANES), lambda b, h, i: (b * nq + i, h)),
            pl.BlockSpec((seq, hp * LANES), lambda b, h, i: (b, h)),
            pl.BlockSpec((seq, hp * LANES), lambda b, h, i: (b, ng + h)),
        ],
        out_specs=pl.BlockSpec((tq, hp * ATT_V_DIM), lambda b, h, i: (b * nq + i, h)),
        out_shape=jax.ShapeDtypeStruct((m, bw), BF16),
        scratch_shapes=[pltpu.VMEM((hp, 2 * tq, LANES), F32), pltpu.VMEM((hp, 2 * tq, 2 * LANES), F32)],
        compiler_params=_cparams(("parallel", "parallel", "arbitrary")),
    )(lam, subln, qx, kvb, kvb)


def _attn_sample_kernel(pt_ref, lam_ref, subln_ref, qm_ref, kn_ref, vn_ref, *rest,
                        n_pages, page, nh, lam_init):
    k_refs = rest[:n_pages]
    v_refs = rest[n_pages:2 * n_pages]
    o_ref = rest[2 * n_pages]
    s_scr = rest[2 * n_pages + 1]
    rows = page * nh
    qm = qm_ref[0]

    def head_mask(width):
        lane = lax.broadcasted_iota(jnp.int32, (nh, width), 1)
        sub = lax.broadcasted_iota(jnp.int32, (nh, width), 0)
        return (lane % nh) == sub

    def scores(kflat, width):
        st = _dot_nt(qm, kflat)
        st = st.reshape(2, nh, width)
        return jnp.sum(jnp.where(head_mask(width)[None], st, 0.0), axis=1)

    for j in range(n_pages):
        kf = k_refs[j][0, 0].reshape(rows, LANES).astype(BF16)
        s_scr[0:2, j * rows:(j + 1) * rows] = scores(kf, rows)
    pad = jnp.zeros((LANES - nh, LANES), F32)
    knp = jnp.concatenate([kn_ref[0, 0], pad], axis=0).astype(BF16)
    s_new = scores(knp, LANES)
    lane = lax.broadcasted_iota(jnp.int32, s_new.shape, 1)
    s_scr[0:2, n_pages * rows:] = jnp.where(lane < nh, s_new, NEG_INF)

    s_all = s_scr[0:2, :]
    ncol = s_all.shape[1] // LANES

    def per_head_allreduce(x, op):
        r = x[:, :LANES]
        for c in range(1, ncol):
            r = op(r, x[:, c * LANES:(c + 1) * LANES])
        sh = nh
        while sh < LANES:
            r = op(r, pltpu.roll(r, sh, 1))
            sh *= 2
        return r

    mx = per_head_allreduce(s_all, jnp.maximum)
    p = jnp.exp(s_all - _lane_tile(mx, ncol))
    den = per_head_allreduce(p, jnp.add)
    pn = p / _lane_tile(den, ncol)
    w = pn[0:1] - _lam_from(lam_ref, lam_init) * pn[1:2]

    def weighted(wj, vflat, width):
        a = jnp.where(head_mask(width), jnp.broadcast_to(wj, (nh, width)), 0.0)
        return _dot(a.astype(BF16), vflat)

    o = jnp.zeros((nh, LANES), F32)
    for j in range(n_pages):
        vf = v_refs[j][0, 0].reshape(rows, LANES).astype(BF16)
        o = o + weighted(w[:, j * rows:(j + 1) * rows], vf, rows)
    vnp = jnp.concatenate([vn_ref[0, 0], pad], axis=0).astype(BF16)
    o = o + weighted(w[:, n_pages * rows:], vnp, LANES)
    o_ref[0] = _rms(o) * subln_ref[...] * (1.0 - lam_init)


def _attn_sample(page_table, lam, subln, qm, kf_all, vf_all, cache_k, cache_v, *, layer, lam_init):
    db, n_pages = page_table.shape
    _, _, page, nh, vd = cache_k.shape
    rows = page * nh

    def page_spec(j):
        return pl.BlockSpec((1, 1, page, nh, vd), lambda b, pt: (layer, pt[b, j], 0, 0, 0))

    kern = functools.partial(_attn_sample_kernel, n_pages=n_pages, page=page, nh=nh, lam_init=lam_init)
    grid_spec = pltpu.PrefetchScalarGridSpec(
        num_scalar_prefetch=1,
        grid=(db,),
        in_specs=[
            pl.BlockSpec(lam.shape, lambda b, pt: (0, 0)),
            pl.BlockSpec((1, vd), lambda b, pt: (0, 0)),
            pl.BlockSpec((1, 2 * nh, LANES), lambda b, pt: (b, 0, 0)),
            pl.BlockSpec((1, 1, nh, vd), lambda b, pt: (layer, b, 0, 0)),
            pl.BlockSpec((1, 1, nh, vd), lambda b, pt: (layer, b, 0, 0)),
        ] + [page_spec(j) for j in range(n_pages)] + [page_spec(j) for j in range(n_pages)],
        out_specs=pl.BlockSpec((1, nh, vd), lambda b, pt: (b, 0, 0)),
        scratch_shapes=[pltpu.VMEM((SUBLANES, n_pages * rows + LANES), F32)],
    )
    return pl.pallas_call(
        kern,
        grid_spec=grid_spec,
        out_shape=jax.ShapeDtypeStruct((db, nh, vd), F32),
        compiler_params=_cparams(("arbitrary",)),
    )(page_table, lam, subln, qm, kf_all, vf_all, *([cache_k] * n_pages), *([cache_v] * n_pages))


def _mlstm_prompt_kernel(q_ref, k_ref, v_ref, mo_ref, gt_ref, gain_ref,
                         h_ref, c_out_ref, n_out_ref, m_out_ref,
                         c_scr, n_scr, m_scr, *, c, dh):
    nmh = N_M_HEADS
    kk = pl.program_id(1)

    @pl.when(kk == 0)
    def _():
        c_scr[...] = jnp.zeros(c_scr.shape, F32)
        n_scr[...] = jnp.zeros(n_scr.shape, F32)
        m_scr[...] = jnp.zeros(m_scr.shape, F32)

    rows = gt_ref[...]
    r_hi, r_lo = _split_hi_lo(rows)
    ti = lax.broadcasted_iota(jnp.int32, (c, c), 0)
    si = lax.broadcasted_iota(jnp.int32, (c, c), 1)
    tril = (si <= ti)
    lower = tril.astype(BF16)
    upper = (ti <= si).astype(BF16)
    eye = (ti == si).astype(BF16)
    cum_rows = _dot(r_hi, upper) + _dot(r_lo, upper)
    cum_cols = _dot_nt(lower, r_hi) + _dot_nt(lower, r_lo)
    id_cols = _dot_nt(eye, r_hi) + _dot_nt(eye, r_lo)

    for h in range(nmh):
        hs = slice(h * dh, (h + 1) * dh)
        b_row = cum_rows[nmh + h:nmh + h + 1, :]
        ig_row = rows[h:h + 1, :]
        b_col = cum_cols[:, nmh + h:nmh + h + 1]
        ig_col = id_cols[:, h:h + 1]
        b_last = b_col[c - 1:c, :]
        m_prev = m_scr[h, 0:1, 0:1]

        d_mat = jnp.where(tril, b_col - b_row + ig_row, NEG_INF)
        inter = b_col + m_prev
        m_t = jnp.maximum(inter, jnp.max(d_mat, axis=1, keepdims=True))
        w_intra = jnp.exp(d_mat - m_t)
        w_inter = jnp.exp(inter - m_t)

        qb = q_ref[:, hs].astype(BF16)
        kf = k_ref[:, hs] * (dh ** -0.5)
        kb = kf.astype(BF16)
        vb = v_ref[:, hs].astype(BF16)
        a = _dot_nt(qb, kb) * w_intra
        c_prev = c_scr[h]
        n_prev = n_scr[h]
        num = _dot(a.astype(BF16), vb) + w_inter * _dot(qb, c_prev.astype(BF16))
        qn = jnp.sum(qb.astype(F32) * n_prev.astype(BF16).astype(F32), axis=1, keepdims=True)
        den = jnp.sum(a, axis=1, keepdims=True) + w_inter * qn
        den = jnp.maximum(jnp.abs(den), jnp.exp(-m_t))
        hh = num / den

        m_new = m_t[c - 1:c, :]
        w_s = jnp.exp(b_last - b_col + ig_col - m_new)
        decay = jnp.exp(b_last + m_prev - m_new)
        kw = kf * w_s
        c_new = decay * c_prev + _dot_tn(kw.astype(BF16), vb)
        n_new = decay * n_prev + jnp.sum(kw, axis=0, keepdims=True)
        c_scr[h] = c_new
        n_scr[h] = n_new
        m_scr[h] = jnp.broadcast_to(m_new, (SUBLANES, LANES))
        c_out_ref[0, h] = c_new
        n_out_ref[0, h] = n_new
        m_out_ref[0, h] = jnp.broadcast_to(m_new, (1, LANES))

        h_ref[:, hs] = (_rms(hh) * gain_ref[:, hs] * _sigmoid(mo_ref[:, hs])).astype(h_ref.dtype)


def _mlstm_prompt(proj, gt, gain, *, batch, seq, c):
    m = proj.shape[0]
    bw = proj.shape[1] // 4
    nmh = N_M_HEADS
    dh = bw // nmh
    nc = seq // c
    kern = functools.partial(_mlstm_prompt_kernel, c=c, dh=dh)

    def col(base):
        return lambda b, k: (b * nc + k, base)

    return pl.pallas_call(
        kern,
        grid=(batch, nc),
        in_specs=[
            pl.BlockSpec((c, bw), col(0)),
            pl.BlockSpec((c, bw), col(1)),
            pl.BlockSpec((c, bw), col(2)),
            pl.BlockSpec((c, bw), col(3)),
            pl.BlockSpec((N_GATES, c), lambda b, k: (0, b * nc + k)),
            pl.BlockSpec((1, bw), lambda b, k: (0, 0)),
        ],
        out_specs=[
            pl.BlockSpec((c, bw), lambda b, k: (b * nc + k, 0)),
            pl.BlockSpec((1, nmh, dh, dh), lambda b, k: (b, 0, 0, 0)),
            pl.BlockSpec((1, nmh, 1, dh), lambda b, k: (b, 0, 0, 0)),
            pl.BlockSpec((1, nmh, 1, LANES), lambda b, k: (b, 0, 0, 0)),
        ],
        out_shape=[
            jax.ShapeDtypeStruct((m, bw), BF16),
            jax.ShapeDtypeStruct((batch, nmh, dh, dh), F32),
            jax.ShapeDtypeStruct((batch, nmh, 1, dh), F32),
            jax.ShapeDtypeStruct((batch, nmh, 1, LANES), F32),
        ],
        scratch_shapes=[pltpu.VMEM((nmh, dh, dh), F32), pltpu.VMEM((nmh, 1, dh), F32),
                        pltpu.VMEM((nmh, SUBLANES, LANES), F32)],
        compiler_params=_cparams(("parallel", "arbitrary")),
    )(proj, proj, proj, proj, gt, gain)


def _mlstm_sample_kernel(*refs, bs, bw, dh, aliased):
    p_ref, g_ref, c_ref, n_ref, m_ref, gain_ref = refs[:6]
    h_ref, c_out_ref, n_out_ref, m_out_ref = refs[6 + (1 if aliased else 0):]
    nmh = N_M_HEADS
    g = g_ref[...]
    m_all = m_ref[0]
    rowid = lax.broadcasted_iota(jnp.int32, (bs, dh), 0)
    for h in range(nmh):
        def cols(base):
            return p_ref[:, base * bw + h * dh: base * bw + (h + 1) * dh]
        q, k, v, mo = cols(0), cols(1) * (dh ** -0.5), cols(2), cols(3)
        ig = g[:, h:h + 1]
        lf = g[:, nmh + h:nmh + h + 1]
        m_prev = m_all[:, h:h + 1]
        m_t = jnp.maximum(lf + m_prev, ig)
        w_i = jnp.exp(ig - m_t)
        w_f = jnp.exp(lf + m_prev - m_t)
        qb = q.astype(BF16)
        qf = qb.astype(F32)
        kb = k.astype(BF16).astype(F32)
        vb = v.astype(BF16)
        n_prev = n_ref[0][:, h * dh:(h + 1) * dh]
        a = jnp.sum(qf * kb, axis=1, keepdims=True) * w_i
        qn = jnp.sum(qf * n_prev.astype(BF16).astype(F32), axis=1, keepdims=True)
        kw = k * w_i
        qc = jnp.zeros((bs, dh), F32)
        for j in range(bs):
            c_prev = c_ref[0, j, h]
            qc = jnp.where(rowid == j, _dot(qb, c_prev.astype(BF16)), qc)
            kwj = jnp.where(rowid == j, kw, 0.0).astype(BF16)
            c_out_ref[0, j, h] = w_f[j:j + 1, :] * c_prev + _dot_tn(kwj, vb)
        num = a * vb.astype(F32) + w_f * qc
        den = a + w_f * qn
        den = jnp.maximum(jnp.abs(den), jnp.exp(-m_t))
        hh = num / den
        n_out_ref[:, h * dh:(h + 1) * dh] = w_f * n_prev + kw
        m_out_ref[:, h:h + 1] = m_t
        gain = gain_ref[:, h * dh:(h + 1) * dh]
        h_ref[:, h * dh:(h + 1) * dh] = _rms(hh) * gain * _sigmoid(mo)


def _mlstm_sample(proj, g, state_c, state_n2, state_m, gain, c_prev_out, *, layer, bs):
    db = proj.shape[0]
    depth = state_c.shape[0]
    bw = proj.shape[1] // 4
    nmh = N_M_HEADS
    dh = bw // nmh
    aliased = c_prev_out is not None
    kern = functools.partial(_mlstm_sample_kernel, bs=bs, bw=bw, dh=dh, aliased=aliased)
    in_specs = [
        pl.BlockSpec((bs, 4 * bw), lambda i: (i, 0)),
        pl.BlockSpec((bs, N_GATES), lambda i: (i, 0)),
        pl.BlockSpec((1, bs, nmh, dh, dh), lambda i: (layer, i, 0, 0, 0)),
        pl.BlockSpec((1, bs, bw), lambda i: (layer, i, 0)),
        pl.BlockSpec((1, bs, nmh), lambda i: (layer, i, 0)),
        pl.BlockSpec((1, bw), lambda i: (0, 0)),
    ]
    args = [proj, g, state_c, state_n2, state_m, gain]
    aliases = {}
    if aliased:
        in_specs.append(_any_spec())
        args.append(c_prev_out)
        aliases = {6: 1}
    return pl.pallas_call(
        kern,
        grid=(db // bs,),
        in_specs=in_specs,
        out_specs=[
            pl.BlockSpec((bs, bw), lambda i: (i, 0)),
            pl.BlockSpec((1, bs, nmh, dh, dh), lambda i: (layer, i, 0, 0, 0)),
            pl.BlockSpec((bs, bw), lambda i: (i, 0)),
            pl.BlockSpec((bs, nmh), lambda i: (i, 0)),
        ],
        out_shape=[
            jax.ShapeDtypeStruct((db, bw), F32),
            jax.ShapeDtypeStruct((depth, db, nmh, dh, dh), F32),
            jax.ShapeDtypeStruct((db, bw), F32),
            jax.ShapeDtypeStruct((db, nmh), F32),
        ],
        input_output_aliases=aliases,
        compiler_params=_cparams(("parallel",)),
    )(*args)


def _pick_tile(n, prefs):
    for t in prefs:
        if n % t == 0:
            return t
    return n


def kernel(x_prompt, x_sample, cache_k, cache_v, state_C, state_n, state_m, page_table, norm_ffn1, ffn1_w_in, ffn1_w_out, norm_mix, w_mix_in, b_gates, lam_q1, lam_k1, lam_q2, lam_k2, attn_subln, mlstm_gain, w_mix_out, norm_ffn2, ffn2_w_in, ffn2_w_out, norm_final):
    batch, seq, d = x_prompt.shape
    db, t_dec, _ = x_sample.shape
    depth = cache_k.shape[0]
    page = cache_k.shape[2]
    nh = cache_k.shape[3]
    past = page_table.shape[1] * page
    bw = d // 2
    nmh = N_M_HEADS
    dh = bw // nmh
    nmain = N_MIX_BLOCKS * bw
    assert t_dec == 1 and nh * ATT_V_DIM == bw and w_mix_in.shape[2] == nmain + N_GATES

    mp = batch * seq
    tm_p = _pick_tile(mp, (512, 256, 128))
    tm_s = db
    tm_f = _pick_tile(mp, (1024, 512, 256, 128))
    tf = 256
    tq = _pick_tile(seq, (512, 256, 128))
    hp = 4 if nh % 4 == 0 else 1
    chunk = _pick_tile(seq, (256, 128))
    bs = SUBLANES

    xp = x_prompt.reshape(mp, d)
    xs = x_sample.reshape(db, d)
    tab = _rope_table()
    gf = norm_final.reshape(1, d)
    state_n2 = state_n.reshape(depth, db, bw)

    w1_in, w1_out = ffn1_w_in, ffn1_w_out
    w2_in, w2_out = ffn2_w_in, ffn2_w_out
    wmix = w_mix_in.astype(BF16)
    wout = w_mix_out.astype(BF16)
    wg_small = lax.optimization_barrier(w_mix_in[:, :, nmain:])

    kf_p = vf_p = kf_s = vf_s = c_s_all = None
    outs_p = [[] for _ in range(3)]
    outs_s = [[] for _ in range(2)]
    for l in range(depth):
        lam_init = 0.8 - 0.6 * math.exp(-0.3 * l)
        last = l == depth - 1
        wgt_hi, wgt_lo = _split_hi_lo(wg_small[l].T)
        wgt = jnp.concatenate([wgt_hi, wgt_lo], axis=0)
        bg = jnp.broadcast_to(b_gates[l].reshape(N_GATES, 1), (N_GATES, LANES))
        lam = jnp.stack([lam_q1[l], lam_k1[l], lam_q2[l], lam_k2[l]])
        subln = attn_subln[l].reshape(1, ATT_V_DIM)
        gain = mlstm_gain[l].reshape(1, bw)
        g1 = norm_ffn1[l].reshape(1, d)
        gm = norm_mix[l].reshape(1, d)
        g2 = norm_ffn2[l].reshape(1, d)

        xp, xs = _ffn(xp, xs, g1, w1_in, w1_out, gf, layer=l, tm=tm_f, tf=tf, final_norm=False)

        proj_p, qx_p, kvb_p, gt_p, kf_p, vf_p = _mixin(
            xp, gm, wmix, wgt, bg, tab, kf_p, vf_p, layer=l, depth=depth, tm=tm_p, seq=seq, pos_offset=0, pos_step=1)
        proj_s, qx_s, kvb_s, gt_s, kf_s, vf_s = _mixin(
            xs, gm, wmix, wgt, bg, tab, kf_s, vf_s, layer=l, depth=depth, tm=tm_s, seq=seq, pos_offset=past, pos_step=0)

        a_p = _attn_prompt(lam, subln, qx_p, kvb_p, batch=batch, seq=seq, tq=tq, hp=hp, lam_init=lam_init)
        h_p, c_p, n_p, m_p = _mlstm_prompt(proj_p, gt_p, gain, batch=batch, seq=seq, c=chunk)

        qm = qx_s.reshape(db, nh, 2, LANES).transpose(0, 2, 1, 3).reshape(db, 2 * nh, LANES)
        a_s = _attn_sample(page_table, lam, subln, qm, kf_s, vf_s, cache_k, cache_v, layer=l, lam_init=lam_init)
        h_s, c_s_all, n_s, m_s = _mlstm_sample(proj_s, gt_s.T, state_C, state_n2, state_m, gain, c_s_all, layer=l, bs=bs)

        xp = _mixout(xp, a_p, h_p, wout, layer=l, tm=tm_p)
        xs = _mixout(xs, a_s.reshape(db, bw), h_s, wout, layer=l, tm=tm_s)

        xp, xs = _ffn(xp, xs, g2, w2_in, w2_out, gf, layer=l, tm=tm_f, tf=tf, final_norm=last)

        outs_p[0].append(c_p)
        outs_p[1].append(n_p.reshape(batch, nmh, dh))
        outs_p[2].append(m_p[:, :, 0, 0])
        outs_s[0].append(n_s.reshape(db, nmh, dh))
        outs_s[1].append(m_s)

    y_prompt = xp.reshape(batch, seq, d)
    y_sample = xs.reshape(db, 1, d)
    return (y_prompt, y_sample,
            kf_p.reshape(depth, batch, seq, nh, ATT_V_DIM), vf_p.reshape(depth, batch, seq, nh, ATT_V_DIM),
            *[jnp.stack(o) for o in outs_p],
            kf_s.reshape(depth, db, 1, nh, ATT_V_DIM), vf_s.reshape(depth, db, 1, nh, ATT_V_DIM),
            c_s_all, *[jnp.stack(o) for o in outs_s])
```

```python
import functools
import math

import jax
import jax.numpy as jnp
import numpy as np
from jax import lax
from jax.experimental import pallas as pl
from jax.experimental.pallas import tpu as pltpu

F32 = jnp.float32
BF16 = jnp.bfloat16

EPS = 1e-6
ATT_HEAD_DIM = 64
ATT_V_DIM = 2 * ATT_HEAD_DIM
ATT_SCALE = ATT_HEAD_DIM ** -0.5
ROT_DIM = ATT_HEAD_DIM // 4
ROPE_THETA = 500000.0
N_M_HEADS = 4
N_GATES = 2 * N_M_HEADS
N_MIX_BLOCKS = 7

LANES = 128
SUBLANES = 8
VMEM_LIMIT_BYTES = 56 * 1024 * 1024

NEG_INF = float("-inf")


def _cparams(semantics):
    return pltpu.CompilerParams(dimension_semantics=semantics, vmem_limit_bytes=VMEM_LIMIT_BYTES)


def _rms(x):
    return x * lax.rsqrt(jnp.mean(x * x, axis=-1, keepdims=True) + EPS)


def _sigmoid(x):
    return 1.0 / (1.0 + jnp.exp(-x))


def _dot(a, b):
    return jnp.dot(a, b, preferred_element_type=F32)


def _dot_nt(a, b):
    return lax.dot_general(a, b, (((1,), (1,)), ((), ())), preferred_element_type=F32)


def _dot_tn(a, b):
    return lax.dot_general(a, b, (((0,), (0,)), ((), ())), preferred_element_type=F32)


def _split_hi_lo(x):
    hi = x.astype(BF16)
    lo = (x - hi.astype(F32)).astype(BF16)
    return hi, lo


def _lane_tile(x, n):
    return jnp.concatenate([x] * n, axis=1)


def _any_spec():
    return pl.BlockSpec(memory_space=pl.ANY)


def _ffn_kernel(*refs, tf, rem, final_norm):
    nsub = tf // LANES
    x_ref, xs_ref, g_ref, wg_ref = refs[:4]
    wu_refs = refs[4:4 + nsub]
    wo_ref, gf_ref, o_ref, os_ref, xn_ref, xsn_ref = refs[4 + nsub:]
    i = pl.program_id(0)
    j = pl.program_id(1)
    nj = pl.num_programs(1)

    def init(src_ref, n_ref, dst_ref):
        x = src_ref[...]
        n_ref[...] = (_rms(x) * g_ref[...]).astype(BF16)
        dst_ref[...] = x

    @pl.when(j == 0)
    def _():
        init(x_ref, xn_ref, o_ref)

    @pl.when((j == 0) & (i == 0))
    def _():
        init(xs_ref, xsn_ref, os_ref)

    def step(width):
        wg = wg_ref[0, :, :width].astype(BF16)
        wu = jnp.concatenate([r[0].astype(BF16) for r in wu_refs[:width // LANES]], axis=1)
        wo = wo_ref[0, :width, :].astype(BF16)

        def rows(n_ref, dst_ref):
            xn = n_ref[...]
            gate = _dot(xn, wg)
            up = _dot(xn, wu)
            act = (gate * _sigmoid(gate)) * up * 0.5
            dst_ref[...] += _dot(act.astype(BF16), wo)

        rows(xn_ref, o_ref)

        @pl.when(i == 0)
        def _():
            rows(xsn_ref, os_ref)

    if rem == tf:
        step(tf)
    else:
        @pl.when(j < nj - 1)
        def _():
            step(tf)

        @pl.when(j == nj - 1)
        def _():
            step(rem)

    if final_norm:
        @pl.when(j == nj - 1)
        def _():
            o_ref[...] = _rms(o_ref[...]) * gf_ref[...]

        @pl.when((j == nj - 1) & (i == 0))
        def _():
            os_ref[...] = _rms(os_ref[...]) * gf_ref[...]


def _ffn(x, xs, g, w_in, w_out, gf, *, layer, tm, tf, final_norm):
    m, d = x.shape
    ms = xs.shape[0]
    f = w_out.shape[1]
    assert f % LANES == 0 and tf % LANES == 0
    nj = (f + tf - 1) // tf
    rem = f - (nj - 1) * tf
    nsub = tf // LANES
    last_blk = 2 * f // LANES - 1

    def up_spec(r):
        return pl.BlockSpec((1, d, LANES),
                            lambda i, j: (layer, 0, jnp.minimum(f // LANES + nsub * j + r, last_blk)))

    return pl.pallas_call(
        functools.partial(_ffn_kernel, tf=tf, rem=rem, final_norm=final_norm),
        grid=(m // tm, nj),
        in_specs=[
            pl.BlockSpec((tm, d), lambda i, j: (i, 0)),
            pl.BlockSpec((ms, d), lambda i, j: (0, 0)),
            pl.BlockSpec((1, d), lambda i, j: (0, 0)),
            pl.BlockSpec((1, d, tf), lambda i, j: (layer, 0, j)),
        ] + [up_spec(r) for r in range(nsub)] + [
            pl.BlockSpec((1, tf, d), lambda i, j: (layer, j, 0)),
            pl.BlockSpec((1, d), lambda i, j: (0, 0)),
        ],
        out_specs=[
            pl.BlockSpec((tm, d), lambda i, j: (i, 0)),
            pl.BlockSpec((ms, d), lambda i, j: (0, 0)),
        ],
        out_shape=[jax.ShapeDtypeStruct((m, d), F32), jax.ShapeDtypeStruct((ms, d), F32)],
        scratch_shapes=[pltpu.VMEM((tm, d), BF16), pltpu.VMEM((ms, d), BF16)],
        compiler_params=_cparams(("arbitrary", "arbitrary")),
    )(x, xs, g, w_in, *([w_in] * nsub), w_out, gf)


def _rope_table():
    half = ROT_DIM // 2
    inv = ROPE_THETA ** (-jnp.arange(0, ROT_DIM, 2, dtype=F32) / ROT_DIM)
    r = np.arange(LANES) % ATT_HEAD_DIM
    inv_l = jnp.where(r < ROT_DIM, inv[r % half], 0.0)
    tab = jnp.zeros((SUBLANES, LANES), F32)
    tab = tab.at[0].set(inv_l)
    tab = tab.at[1].set(jnp.asarray(np.where(r < half, -1.0, 0.0), F32))
    tab = tab.at[2].set(jnp.asarray(np.where((r >= half) & (r < ROT_DIM), 1.0, 0.0), F32))
    return tab


def _mixin_kernel(*refs, tm, bw, seq, pos_offset, pos_step, aliased):
    x_ref, g_ref, w_ref, wgt_ref, bg_ref, tab_ref = refs[:6]
    outs = refs[6 + (2 if aliased else 0):]
    proj_ref, qx_ref, kvb_ref, gt_ref, kf_ref, vf_ref, xn_ref, cs_ref, base_ref, raw_a, raw_b = outs
    assert N_MIX_BLOCKS == 7
    i = pl.program_id(0)
    j = pl.program_id(1)
    half = ROT_DIM // 2
    ncol = bw // LANES

    def matmul():
        return _dot(xn_ref[...], w_ref[0])

    def put_raw(raw_ref, y):
        for c in range(ncol):
            raw_ref[c * tm:(c + 1) * tm, :] = y[:, c * LANES:(c + 1) * LANES]

    def rotate(raw_ref):
        cols = []
        for c in range(ncol):
            yc = raw_ref[c * tm:(c + 1) * tm, :]
            cols.append(yc * cs_ref[0]
                        + pltpu.roll(yc, LANES - half, 1) * cs_ref[1]
                        + pltpu.roll(yc, half, 1) * cs_ref[2])
        return cols

    @pl.when((i == 0) & (j == 0))
    def _():
        off = (pos_step * lax.broadcasted_iota(jnp.int32, (tm, LANES), 0)).astype(F32)
        ang_b = off * tab_ref[0:1, :]
        base_ref[0] = jnp.cos(ang_b)
        base_ref[1] = jnp.sin(ang_b)

    @pl.when(j == 0)
    def _():
        xn = _rms(x_ref[...]) * g_ref[...]
        x_hi, x_lo = _split_hi_lo(xn)
        xn_ref[...] = x_hi
        put_raw(raw_a, _dot(x_hi, w_ref[0]))
        row0 = ((i * tm) % seq) * pos_step + pos_offset
        ang_a = row0.astype(F32) * tab_ref[0:1, :]
        cos_a = jnp.cos(ang_a)
        sin_a = jnp.sin(ang_a)
        cos = cos_a * base_ref[0] - sin_a * base_ref[1]
        sin = sin_a * base_ref[0] + cos_a * base_ref[1]
        cs_ref[0] = cos
        cs_ref[1] = sin * tab_ref[1:2, :]
        cs_ref[2] = sin * tab_ref[2:3, :]
        wst = wgt_ref[...]
        r1 = _dot_nt(wst, x_hi)
        r2 = _dot_nt(wst, x_lo)
        pre = r1[:N_GATES] + r1[N_GATES:] + r2[:N_GATES] + bg_ref[...][:, :1]
        logsig = jnp.minimum(pre, 0.0) - jnp.log1p(jnp.exp(-jnp.abs(pre)))
        is_f = lax.broadcasted_iota(jnp.int32, pre.shape, 0) >= N_M_HEADS
        gt_ref[...] = jnp.where(is_f, logsig, pre)

    @pl.when(j == 1)
    def _():
        put_raw(raw_b, matmul())
        cols = rotate(raw_a)
        lane = lax.broadcasted_iota(jnp.int32, (tm, LANES), 1)
        first = lane < ATT_HEAD_DIM
        for c, yc in enumerate(cols):
            yc = yc * ATT_SCALE
            qx_ref[:, (2 * c) * LANES:(2 * c + 1) * LANES] = jnp.where(first, yc, 0.0).astype(BF16)
            qx_ref[:, (2 * c + 1) * LANES:(2 * c + 2) * LANES] = jnp.where(first, 0.0, yc).astype(BF16)

    @pl.when(j == 2)
    def _():
        put_raw(raw_a, matmul())
        cols = rotate(raw_b)
        for c, yc in enumerate(cols):
            raw_b[c * tm:(c + 1) * tm, :] = yc
            kvb_ref[:, c * LANES:(c + 1) * LANES] = yc.astype(BF16)

    def gather_tokens(dst_ref, raw_ref, part):
        lo, hi = (0, tm // 2) if part == 0 else (tm // 2, tm)
        blk = jnp.stack([raw_ref[c * tm + lo:c * tm + hi, :] for c in range(ncol)], axis=0)
        dst_ref[0, lo:hi] = jnp.swapaxes(blk, 0, 1)

    @pl.when(j == 3)
    def _():
        proj_ref[...] = matmul()
        for c in range(ncol):
            kvb_ref[:, c * LANES:(c + 1) * LANES] = raw_a[c * tm:(c + 1) * tm, :].astype(BF16)
        gather_tokens(kf_ref, raw_b, 0)

    @pl.when(j == 4)
    def _():
        proj_ref[...] = matmul()
        gather_tokens(kf_ref, raw_b, 1)

    @pl.when(j == 5)
    def _():
        proj_ref[...] = matmul()
        gather_tokens(vf_ref, raw_a, 0)

    @pl.when(j == 6)
    def _():
        proj_ref[...] = matmul()
        gather_tokens(vf_ref, raw_a, 1)


def _mixin(x, g, w_all, wgt, bg, tab, kf_prev, vf_prev, *, layer, depth, tm, seq, pos_offset, pos_step):
    m, d = x.shape
    bw = d // 2
    nh = bw // ATT_V_DIM
    aliased = kf_prev is not None
    kern = functools.partial(_mixin_kernel, tm=tm, bw=bw, seq=seq, pos_offset=pos_offset,
                             pos_step=pos_step, aliased=aliased)
    in_specs = [
        pl.BlockSpec((tm, d), lambda i, j: (i, 0)),
        pl.BlockSpec((1, d), lambda i, j: (0, 0)),
        pl.BlockSpec((1, d, bw), lambda i, j: (layer, 0, j)),
        pl.BlockSpec((2 * N_GATES, d), lambda i, j: (0, 0)),
        pl.BlockSpec((N_GATES, LANES), lambda i, j: (0, 0)),
        pl.BlockSpec((SUBLANES, LANES), lambda i, j: (0, 0)),
    ]
    args = [x, g, w_all, wgt, bg, tab]
    aliases = {}
    if aliased:
        in_specs += [_any_spec(), _any_spec()]
        args += [kf_prev, vf_prev]
        aliases = {6: 4, 7: 5}
    return pl.pallas_call(
        kern,
        grid=(m // tm, N_MIX_BLOCKS),
        in_specs=in_specs,
        out_specs=[
            pl.BlockSpec((tm, bw), lambda i, j: (i, jnp.clip(j - 3, 0, 3))),
            pl.BlockSpec((tm, 2 * bw), lambda i, j: (i, 0)),
            pl.BlockSpec((tm, bw), lambda i, j: (i, jnp.clip(j - 2, 0, 1))),
            pl.BlockSpec((N_GATES, tm), lambda i, j: (0, i)),
            pl.BlockSpec((1, tm, nh, ATT_V_DIM), lambda i, j: (layer, i, 0, 0)),
            pl.BlockSpec((1, tm, nh, ATT_V_DIM), lambda i, j: (layer, i, 0, 0)),
        ],
        out_shape=[
            jax.ShapeDtypeStruct((m, 4 * bw), F32),
            jax.ShapeDtypeStruct((m, 2 * bw), BF16),
            jax.ShapeDtypeStruct((m, 2 * bw), BF16),
            jax.ShapeDtypeStruct((N_GATES, m), F32),
            jax.ShapeDtypeStruct((depth, m, nh, ATT_V_DIM), F32),
            jax.ShapeDtypeStruct((depth, m, nh, ATT_V_DIM), F32),
        ],
        scratch_shapes=[pltpu.VMEM((tm, d), BF16), pltpu.VMEM((3, tm, LANES), F32),
                        pltpu.VMEM((2, tm, LANES), F32),
                        pltpu.VMEM((bw // LANES * tm, LANES), F32), pltpu.VMEM((bw // LANES * tm, LANES), F32)],
        input_output_aliases=aliases,
        compiler_params=_cparams(("arbitrary", "arbitrary")),
    )(*args)


def _mixout_kernel(x_ref, a_ref, h_ref, wa_ref, wh_ref, o_ref):
    o_ref[...] = (x_ref[...] + _dot(a_ref[...].astype(BF16), wa_ref[0])
                  + _dot(h_ref[...].astype(BF16), wh_ref[0]))


def _mixout(x, a, h, w_all, *, layer, tm):
    m, d = x.shape
    bw = a.shape[1]
    return pl.pallas_call(
        _mixout_kernel,
        grid=(m // tm,),
        in_specs=[
            pl.BlockSpec((tm, d), lambda i: (i, 0)),
            pl.BlockSpec((tm, bw), lambda i: (i, 0)),
            pl.BlockSpec((tm, bw), lambda i: (i, 0)),
            pl.BlockSpec((1, bw, d), lambda i: (layer, 0, 0)),
            pl.BlockSpec((1, bw, d), lambda i: (layer, 1, 0)),
        ],
        out_specs=pl.BlockSpec((tm, d), lambda i: (i, 0)),
        out_shape=jax.ShapeDtypeStruct((m, d), F32),
        compiler_params=_cparams(("parallel",)),
    )(x, a, h, w_all, w_all)


def _lam_from(lam_ref, lam_init):
    lv = lam_ref[...]
    s1 = jnp.sum(lv[0:1] * lv[1:2], axis=-1, keepdims=True)
    s2 = jnp.sum(lv[2:3] * lv[3:4], axis=-1, keepdims=True)
    return jnp.exp(s1) - jnp.exp(s2) + lam_init


def _attn_prompt_kernel(lam_ref, subln_ref, q_ref, k_ref, v_ref, o_ref, m_scr, acc_scr, *, tq, hp, lam_init):
    qi = pl.program_id(2)
    tk = tq
    m_scr[...] = jnp.full(m_scr.shape, NEG_INF, F32)
    acc_scr[...] = jnp.zeros(acc_scr.shape, F32)
    ones = jnp.ones((tk, LANES), BF16)
    qs = []
    for hh in range(hp):
        q2 = q_ref[:, hh * 2 * LANES:(hh + 1) * 2 * LANES]
        qs.append(jnp.concatenate([q2[:, :LANES], q2[:, LANES:]], axis=0))

    def chunk(j, masked):
        start = pl.multiple_of(j * tk, tk)
        for hh in range(hp):
            kc = k_ref[pl.ds(start, tk), hh * LANES:(hh + 1) * LANES]
            vc = v_ref[pl.ds(start, tk), hh * LANES:(hh + 1) * LANES]
            s = _dot_nt(qs[hh], kc)
            if masked:
                row = lax.broadcasted_iota(jnp.int32, s.shape, 0)
                row = jnp.where(row >= tq, row - tq, row)
                col = lax.broadcasted_iota(jnp.int32, s.shape, 1)
                s = jnp.where(col <= row, s, NEG_INF)
            m_prev = m_scr[hh]
            m_new = jnp.maximum(m_prev, jnp.max(s, axis=1, keepdims=True))
            alpha = jnp.exp(m_prev - m_new)
            p = jnp.exp(s - _lane_tile(m_new, tk // LANES))
            vx = jnp.concatenate([vc, ones], axis=1)
            acc_scr[hh] = _lane_tile(alpha, 2) * acc_scr[hh] + _dot(p.astype(BF16), vx)
            m_scr[hh] = m_new

    def body(p, carry):
        for u in range(4):
            chunk(4 * p + u, False)
        return carry

    lax.fori_loop(0, qi // 4, body, 0)
    done = (qi // 4) * 4

    @pl.when(qi % 4 >= 2)
    def _():
        chunk(done, False)
        chunk(done + 1, False)

    @pl.when(qi % 2 == 1)
    def _():
        chunk(qi - 1, False)

    chunk(qi, True)

    lam = _lam_from(lam_ref, lam_init)
    for hh in range(hp):
        acc = acc_scr[hh]
        o1 = acc[:tq, :LANES] / acc[:tq, LANES:]
        o2 = acc[tq:, :LANES] / acc[tq:, LANES:]
        o = o1 - lam * o2
        o_ref[:, hh * LANES:(hh + 1) * LANES] = (_rms(o) * subln_ref[...] * (1.0 - lam_init)).astype(o_ref.dtype)


def _attn_prompt(lam, subln, qx, kvb, *, batch, seq, tq, hp, lam_init):
    m = qx.shape[0]
    bw = kvb.shape[1] // 2
    nh = bw // ATT_V_DIM
    ng = nh // hp
    nq = seq // tq
    kern = functools.partial(_attn_prompt_kernel, tq=tq, hp=hp, lam_init=lam_init)
    return pl.pallas_call(
        kern,
        grid=(batch, ng, nq),
        in_specs=[
            pl.BlockSpec(lam.shape, lambda b, h, i: (0, 0)),
            pl.BlockSpec((1, ATT_V_DIM), lambda b, h, i: (0, 0)),
            pl.BlockSpec((tq, hp * 2 * LANES), lambda b, h, i: (b * nq + i, h)),
            pl.BlockSpec((seq, hp * LANES), lambda b, h, i: (b, h)),
            pl.BlockSpec((seq, hp * LANES), lambda b, h, i: (b, ng + h)),
        ],
        out_specs=pl.BlockSpec((tq, hp * ATT_V_DIM), lambda b, h, i: (b * nq + i, h)),
        out_shape=jax.ShapeDtypeStruct((m, bw), BF16),
        scratch_shapes=[pltpu.VMEM((hp, 2 * tq, LANES), F32), pltpu.VMEM((hp, 2 * tq, 2 * LANES), F32)],
        compiler_params=_cparams(("parallel", "parallel", "arbitrary")),
    )(lam, subln, qx, kvb, kvb)


def _attn_sample_kernel(pt_ref, lam_ref, subln_ref, qm_ref, kn_ref, vn_ref, *rest,
                        n_pages, page, nh, lam_init):
    k_refs = rest[:n_pages]
    v_refs = rest[n_pages:2 * n_pages]
    o_ref = rest[2 * n_pages]
    s_scr = rest[2 * n_pages + 1]
    rows = page * nh
    qm = qm_ref[0]

    def head_mask(width):
        lane = lax.broadcasted_iota(jnp.int32, (nh, width), 1)
        sub = lax.broadcasted_iota(jnp.int32, (nh, width), 0)
        return (lane % nh) == sub

    def scores(kflat, width):
        st = _dot_nt(qm, kflat)
        st = st.reshape(2, nh, width)
        return jnp.sum(jnp.where(head_mask(width)[None], st, 0.0), axis=1)

    for j in range(n_pages):
        kf = k_refs[j][0, 0].reshape(rows, LANES).astype(BF16)
        s_scr[0:2, j * rows:(j + 1) * rows] = scores(kf, rows)
    pad = jnp.zeros((LANES - nh, LANES), F32)
    knp = jnp.concatenate([kn_ref[0, 0], pad], axis=0).astype(BF16)
    s_new = scores(knp, LANES)
    lane = lax.broadcasted_iota(jnp.int32, s_new.shape, 1)
    s_scr[0:2, n_pages * rows:] = jnp.where(lane < nh, s_new, NEG_INF)

    s_all = s_scr[0:2, :]
    ncol = s_all.shape[1] // LANES

    def per_head_allreduce(x, op):
        r = x[:, :LANES]
        for c in range(1, ncol):
            r = op(r, x[:, c * LANES:(c + 1) * LANES])
        sh = nh
        while sh < LANES:
            r = op(r, pltpu.roll(r, sh, 1))
            sh *= 2
        return r

    mx = per_head_allreduce(s_all, jnp.maximum)
    p = jnp.exp(s_all - _lane_tile(mx, ncol))
    den = per_head_allreduce(p, jnp.add)
    pn = p / _lane_tile(den, ncol)
    w = pn[0:1] - _lam_from(lam_ref, lam_init) * pn[1:2]

    def weighted(wj, vflat, width):
        a = jnp.where(head_mask(width), jnp.broadcast_to(wj, (nh, width)), 0.0)
        return _dot(a.astype(BF16), vflat)

    o = jnp.zeros((nh, LANES), F32)
    for j in range(n_pages):
        vf = v_refs[j][0, 0].reshape(rows, LANES).astype(BF16)
        o = o + weighted(w[:, j * rows:(j + 1) * rows], vf, rows)
    vnp = jnp.concatenate([vn_ref[0, 0], pad], axis=0).astype(BF16)
    o = o + weighted(w[:, n_pages * rows:], vnp, LANES)
    o_ref[0] = _rms(o) * subln_ref[...] * (1.0 - lam_init)


def _attn_sample(page_table, lam, subln, qm, kf_all, vf_all, cache_k, cache_v, *, layer, lam_init):
    db, n_pages = page_table.shape
    _, _, page, nh, vd = cache_k.shape
    rows = page * nh

    def page_spec(j):
        return pl.BlockSpec((1, 1, page, nh, vd), lambda b, pt: (layer, pt[b, j], 0, 0, 0))

    kern = functools.partial(_attn_sample_kernel, n_pages=n_pages, page=page, nh=nh, lam_init=lam_init)
    grid_spec = pltpu.PrefetchScalarGridSpec(
        num_scalar_prefetch=1,
        grid=(db,),
        in_specs=[
            pl.BlockSpec(lam.shape, lambda b, pt: (0, 0)),
            pl.BlockSpec((1, vd), lambda b, pt: (0, 0)),
            pl.BlockSpec((1, 2 * nh, LANES), lambda b, pt: (b, 0, 0)),
            pl.BlockSpec((1, 1, nh, vd), lambda b, pt: (layer, b, 0, 0)),
            pl.BlockSpec((1, 1, nh, vd), lambda b, pt: (layer, b, 0, 0)),
        ] + [page_spec(j) for j in range(n_pages)] + [page_spec(j) for j in range(n_pages)],
        out_specs=pl.BlockSpec((1, nh, vd), lambda b, pt: (b, 0, 0)),
        scratch_shapes=[pltpu.VMEM((SUBLANES, n_pages * rows + LANES), F32)],
    )
    return pl.pallas_call(
        kern,
        grid_spec=grid_spec,
        out_shape=jax.ShapeDtypeStruct((db, nh, vd), F32),
        compiler_params=_cparams(("arbitrary",)),
    )(page_table, lam, subln, qm, kf_all, vf_all, *([cache_k] * n_pages), *([cache_v] * n_pages))


def _mlstm_prompt_kernel(q_ref, k_ref, v_ref, mo_ref, gt_ref, gain_ref,
                         h_ref, c_out_ref, n_out_ref, m_out_ref,
                         c_scr, n_scr, m_scr, *, c, dh):
    nmh = N_M_HEADS
    kk = pl.program_id(1)

    @pl.when(kk == 0)
    def _():
        c_scr[...] = jnp.zeros(c_scr.shape, F32)
        n_scr[...] = jnp.zeros(n_scr.shape, F32)
        m_scr[...] = jnp.zeros(m_scr.shape, F32)

    rows = gt_ref[...]
    r_hi, r_lo = _split_hi_lo(rows)
    ti = lax.broadcasted_iota(jnp.int32, (c, c), 0)
    si = lax.broadcasted_iota(jnp.int32, (c, c), 1)
    tril = (si <= ti)
    lower = tril.astype(BF16)
    upper = (ti <= si).astype(BF16)
    eye = (ti == si).astype(BF16)
    cum_rows = _dot(r_hi, upper) + _dot(r_lo, upper)
    cum_cols = _dot_nt(lower, r_hi) + _dot_nt(lower, r_lo)
    id_cols = _dot_nt(eye, r_hi) + _dot_nt(eye, r_lo)

    for h in range(nmh):
        hs = slice(h * dh, (h + 1) * dh)
        b_row = cum_rows[nmh + h:nmh + h + 1, :]
        ig_row = rows[h:h + 1, :]
        b_col = cum_cols[:, nmh + h:nmh + h + 1]
        ig_col = id_cols[:, h:h + 1]
        b_last = b_col[c - 1:c, :]
        m_prev = m_scr[h, 0:1, 0:1]

        d_mat = jnp.where(tril, b_col - b_row + ig_row, NEG_INF)
        inter = b_col + m_prev
        m_t = jnp.maximum(inter, jnp.max(d_mat, axis=1, keepdims=True))
        w_intra = jnp.exp(d_mat - m_t)
        w_inter = jnp.exp(inter - m_t)

        qb = q_ref[:, hs].astype(BF16)
        kf = k_ref[:, hs] * (dh ** -0.5)
        kb = kf.astype(BF16)
        vb = v_ref[:, hs].astype(BF16)
        a = _dot_nt(qb, kb) * w_intra
        c_prev = c_scr[h]
        n_prev = n_scr[h]
        num = _dot(a.astype(BF16), vb) + w_inter * _dot(qb, c_prev.astype(BF16))
        qn = jnp.sum(qb.astype(F32) * n_prev.astype(BF16).astype(F32), axis=1, keepdims=True)
        den = jnp.sum(a, axis=1, keepdims=True) + w_inter * qn
        den = jnp.maximum(jnp.abs(den), jnp.exp(-m_t))
        hh = num / den

        m_new = m_t[c - 1:c, :]
        w_s = jnp.exp(b_last - b_col + ig_col - m_new)
        decay = jnp.exp(b_last + m_prev - m_new)
        kw = kf * w_s
        c_new = decay * c_prev + _dot_tn(kw.astype(BF16), vb)
        n_new = decay * n_prev + jnp.sum(kw, axis=0, keepdims=True)
        c_scr[h] = c_new
        n_scr[h] = n_new
        m_scr[h] = jnp.broadcast_to(m_new, (SUBLANES, LANES))
        c_out_ref[0, h] = c_new
        n_out_ref[0, h] = n_new
        m_out_ref[0, h] = jnp.broadcast_to(m_new, (1, LANES))

        h_ref[:, hs] = (_rms(hh) * gain_ref[:, hs] * _sigmoid(mo_ref[:, hs])).astype(h_ref.dtype)


def _mlstm_prompt(proj, gt, gain, *, batch, seq, c):
    m = proj.shape[0]
    bw = proj.shape[1] // 4
    nmh = N_M_HEADS
    dh = bw // nmh
    nc = seq // c
    kern = functools.partial(_mlstm_prompt_kernel, c=c, dh=dh)

    def col(base):
        return lambda b, k: (b * nc + k, base)

    return pl.pallas_call(
        kern,
        grid=(batch, nc),
        in_specs=[
            pl.BlockSpec((c, bw), col(0)),
            pl.BlockSpec((c, bw), col(1)),
            pl.BlockSpec((c, bw), col(2)),
            pl.BlockSpec((c, bw), col(3)),
            pl.BlockSpec((N_GATES, c), lambda b, k: (0, b * nc + k)),
            pl.BlockSpec((1, bw), lambda b, k: (0, 0)),
        ],
        out_specs=[
            pl.BlockSpec((c, bw), lambda b, k: (b * nc + k, 0)),
            pl.BlockSpec((1, nmh, dh, dh), lambda b, k: (b, 0, 0, 0)),
            pl.BlockSpec((1, nmh, 1, dh), lambda b, k: (b, 0, 0, 0)),
            pl.BlockSpec((1, nmh, 1, LANES), lambda b, k: (b, 0, 0, 0)),
        ],
        out_shape=[
            jax.ShapeDtypeStruct((m, bw), BF16),
            jax.ShapeDtypeStruct((batch, nmh, dh, dh), F32),
            jax.ShapeDtypeStruct((batch, nmh, 1, dh), F32),
            jax.ShapeDtypeStruct((batch, nmh, 1, LANES), F32),
        ],
        scratch_shapes=[pltpu.VMEM((nmh, dh, dh), F32), pltpu.VMEM((nmh, 1, dh), F32),
                        pltpu.VMEM((nmh, SUBLANES, LANES), F32)],
        compiler_params=_cparams(("parallel", "arbitrary")),
    )(proj, proj, proj, proj, gt, gain)


def _mlstm_sample_kernel(*refs, bs, bw, dh, aliased):
    p_ref, g_ref, c_ref, n_ref, m_ref, gain_ref = refs[:6]
    h_ref, c_out_ref, n_out_ref, m_out_ref = refs[6 + (1 if aliased else 0):]
    nmh = N_M_HEADS
    g = g_ref[...]
    m_all = m_ref[0]
    rowid = lax.broadcasted_iota(jnp.int32, (bs, dh), 0)
    for h in range(nmh):
        def cols(base):
            return p_ref[:, base * bw + h * dh: base * bw + (h + 1) * dh]
        q, k, v, mo = cols(0), cols(1) * (dh ** -0.5), cols(2), cols(3)
        ig = g[:, h:h + 1]
        lf = g[:, nmh + h:nmh + h + 1]
        m_prev = m_all[:, h:h + 1]
        m_t = jnp.maximum(lf + m_prev, ig)
        w_i = jnp.exp(ig - m_t)
        w_f = jnp.exp(lf + m_prev - m_t)
        qb = q.astype(BF16)
        qf = qb.astype(F32)
        kb = k.astype(BF16).astype(F32)
        vb = v.astype(BF16)
        n_prev = n_ref[0][:, h * dh:(h + 1) * dh]
        a = jnp.sum(qf * kb, axis=1, keepdims=True) * w_i
        qn = jnp.sum(qf * n_prev.astype(BF16).astype(F32), axis=1, keepdims=True)
        kw = k * w_i
        qc = jnp.zeros((bs, dh), F32)
        for j in range(bs):
            c_prev = c_ref[0, j, h]
            qc = jnp.where(rowid == j, _dot(qb, c_prev.astype(BF16)), qc)
            kwj = jnp.where(rowid == j, kw, 0.0).astype(BF16)
            c_out_ref[0, j, h] = w_f[j:j + 1, :] * c_prev + _dot_tn(kwj, vb)
        num = a * vb.astype(F32) + w_f * qc
        den = a + w_f * qn
        den = jnp.maximum(jnp.abs(den), jnp.exp(-m_t))
        hh = num / den
        n_out_ref[:, h * dh:(h + 1) * dh] = w_f * n_prev + kw
        m_out_ref[:, h:h + 1] = m_t
        gain = gain_ref[:, h * dh:(h + 1) * dh]
        h_ref[:, h * dh:(h + 1) * dh] = _rms(hh) * gain * _sigmoid(mo)


def _mlstm_sample(proj, g, state_c, state_n2, state_m, gain, c_prev_out, *, layer, bs):
    db = proj.shape[0]
    depth = state_c.shape[0]
    bw = proj.shape[1] // 4
    nmh = N_M_HEADS
    dh = bw // nmh
    aliased = c_prev_out is not None
    kern = functools.partial(_mlstm_sample_kernel, bs=bs, bw=bw, dh=dh, aliased=aliased)
    in_specs = [
        pl.BlockSpec((bs, 4 * bw), lambda i: (i, 0)),
        pl.BlockSpec((bs, N_GATES), lambda i: (i, 0)),
        pl.BlockSpec((1, bs, nmh, dh, dh), lambda i: (layer, i, 0, 0, 0)),
        pl.BlockSpec((1, bs, bw), lambda i: (layer, i, 0)),
        pl.BlockSpec((1, bs, nmh), lambda i: (layer, i, 0)),
        pl.BlockSpec((1, bw), lambda i: (0, 0)),
    ]
    args = [proj, g, state_c, state_n2, state_m, gain]
    aliases = {}
    if aliased:
        in_specs.append(_any_spec())
        args.append(c_prev_out)
        aliases = {6: 1}
    return pl.pallas_call(
        kern,
        grid=(db // bs,),
        in_specs=in_specs,
        out_specs=[
            pl.BlockSpec((bs, bw), lambda i: (i, 0)),
            pl.BlockSpec((1, bs, nmh, dh, dh), lambda i: (layer, i, 0, 0, 0)),
            pl.BlockSpec((bs, bw), lambda i: (i, 0)),
            pl.BlockSpec((bs, nmh), lambda i: (i, 0)),
        ],
        out_shape=[
            jax.ShapeDtypeStruct((db, bw), F32),
            jax.ShapeDtypeStruct((depth, db, nmh, dh, dh), F32),
            jax.ShapeDtypeStruct((db, bw), F32),
            jax.ShapeDtypeStruct((db, nmh), F32),
        ],
        input_output_aliases=aliases,
        compiler_params=_cparams(("parallel",)),
    )(*args)


def _pick_tile(n, prefs):
    for t in prefs:
        if n % t == 0:
            return t
    return n


def kernel(x_prompt, x_sample, cache_k, cache_v, state_C, state_n, state_m, page_table, norm_ffn1, ffn1_w_in, ffn1_w_out, norm_mix, w_mix_in, b_gates, lam_q1, lam_k1, lam_q2, lam_k2, attn_subln, mlstm_gain, w_mix_out, norm_ffn2, ffn2_w_in, ffn2_w_out, norm_final):
    batch, seq, d = x_prompt.shape
    db, t_dec, _ = x_sample.shape
    depth = cache_k.shape[0]
    page = cache_k.shape[2]
    nh = cache_k.shape[3]
    past = page_table.shape[1] * page
    bw = d // 2
    nmh = N_M_HEADS
    dh = bw // nmh
    nmain = N_MIX_BLOCKS * bw
    assert t_dec == 1 and nh * ATT_V_DIM == bw and w_mix_in.shape[2] == nmain + N_GATES

    mp = batch * seq
    tm_p = _pick_tile(mp, (512, 256, 128))
    tm_s = db
    tm_f = _pick_tile(mp, (1024, 512, 256, 128))
    tf = 256
    tq = _pick_tile(seq, (512, 256, 128))
    hp = 4 if nh % 4 == 0 else 1
    chunk = _pick_tile(seq, (256, 128))
    bs = SUBLANES

    xp = x_prompt.reshape(mp, d)
    xs = x_sample.reshape(db, d)
    tab = _rope_table()
    gf = norm_final.reshape(1, d)
    state_n2 = state_n.reshape(depth, db, bw)

    w1_in, w1_out = ffn1_w_in, ffn1_w_out
    w2_in, w2_out = ffn2_w_in, ffn2_w_out
    wmix = w_mix_in.astype(BF16)
    wout = w_mix_out.astype(BF16)
    wg_small = lax.optimization_barrier(w_mix_in[:, :, nmain:])

    kf_p = vf_p = kf_s = vf_s = c_s_all = None
    outs_p = [[] for _ in range(3)]
    outs_s = [[] for _ in range(2)]
    for l in range(depth):
        lam_init = 0.8 - 0.6 * math.exp(-0.3 * l)
        last = l == depth - 1
        wgt_hi, wgt_lo = _split_hi_lo(wg_small[l].T)
        wgt = jnp.concatenate([wgt_hi, wgt_lo], axis=0)
        bg = jnp.broadcast_to(b_gates[l].reshape(N_GATES, 1), (N_GATES, LANES))
        lam = jnp.stack([lam_q1[l], lam_k1[l], lam_q2[l], lam_k2[l]])
        subln = attn_subln[l].reshape(1, ATT_V_DIM)
        gain = mlstm_gain[l].reshape(1, bw)
        g1 = norm_ffn1[l].reshape(1, d)
        gm = norm_mix[l].reshape(1, d)
        g2 = norm_ffn2[l].reshape(1, d)

        xp, xs = _ffn(xp, xs, g1, w1_in, w1_out, gf, layer=l, tm=tm_f, tf=tf, final_norm=False)

        proj_p, qx_p, kvb_p, gt_p, kf_p, vf_p = _mixin(
            xp, gm, wmix, wgt, bg, tab, kf_p, vf_p, layer=l, depth=depth, tm=tm_p, seq=seq, pos_offset=0, pos_step=1)
        proj_s, qx_s, kvb_s, gt_s, kf_s, vf_s = _mixin(
            xs, gm, wmix, wgt, bg, tab, kf_s, vf_s, layer=l, depth=depth, tm=tm_s, seq=seq, pos_offset=past, pos_step=0)

        a_p = _attn_prompt(lam, subln, qx_p, kvb_p, batch=batch, seq=seq, tq=tq, hp=hp, lam_init=lam_init)
        h_p, c_p, n_p, m_p = _mlstm_prompt(proj_p, gt_p, gain, batch=batch, seq=seq, c=chunk)

        qm = qx_s.reshape(db, nh, 2, LANES).transpose(0, 2, 1, 3).reshape(db, 2 * nh, LANES)
        a_s = _attn_sample(page_table, lam, subln, qm, kf_s, vf_s, cache_k, cache_v, layer=l, lam_init=lam_init)
        h_s, c_s_all, n_s, m_s = _mlstm_sample(proj_s, gt_s.T, state_C, state_n2, state_m, gain, c_s_all, layer=l, bs=bs)

        xp = _mixout(xp, a_p, h_p, wout, layer=l, tm=tm_p)
        xs = _mixout(xs, a_s.reshape(db, bw), h_s, wout, layer=l, tm=tm_s)

        xp, xs = _ffn(xp, xs, g2, w2_in, w2_out, gf, layer=l, tm=tm_f, tf=tf, final_norm=last)

        outs_p[0].append(c_p)
        outs_p[1].append(n_p.reshape(batch, nmh, dh))
        outs_p[2].append(m_p[:, :, 0, 0])
        outs_s[0].append(n_s.reshape(db, nmh, dh))
        outs_s[1].append(m_s)

    y_prompt = xp.reshape(batch, seq, d)
    y_sample = xs.reshape(db, 1, d)
    return (y_prompt, y_sample,
            kf_p.reshape(depth, batch, seq, nh, ATT_V_DIM), vf_p.reshape(depth, batch, seq, nh, ATT_V_DIM),
            *[jnp.stack(o) for o in outs_p],
            kf_s.reshape(depth, db, 1, nh, ATT_V_DIM), vf_s.reshape(depth, db, 1, nh, ATT_V_DIM),
            c_s_all, *[jnp.stack(o) for o in outs_s])
```
